```python
import math
import jax
import jax.numpy as jnp
from jax import lax
import numpy as np

D_MODEL = 4096
BATCH = 1
SEQ = 8192
DEPTH = 4

CTX_LEN = 256
GRID_W = 64
HEAD_DIM = 128
MIX_WIDTH = D_MODEL
A_WIDTH = MIX_WIDTH // 2
CONV_WIDTH = 3
N_DIFF_HEADS = (MIX_WIDTH // 2) // HEAD_DIM
DIFF_QK_DIM = HEAD_DIM // 2
DIFF_V_DIM = HEAD_DIM
DIFF_QK_WIDTH = N_DIFF_HEADS * 2 * DIFF_QK_DIM
DIFF_WIDTH = N_DIFF_HEADS * DIFF_V_DIM
EV_IN_WIDTH = 3 * A_WIDTH + 2 * DIFF_QK_WIDTH + DIFF_WIDTH
EV_SPLITS = (A_WIDTH, 2 * A_WIDTH, 3 * A_WIDTH, 3 * A_WIDTH + DIFF_QK_WIDTH, 3 * A_WIDTH + 2 * DIFF_QK_WIDTH)
POOL_WIDTH = MIX_WIDTH // 2
POOL_WINDOWS = (2, 4, 8, 16)
N_POOL_GROUPS = len(POOL_WINDOWS)
POOL_GROUP = POOL_WIDTH // N_POOL_GROUPS
N_MLA_HEADS = (MIX_WIDTH // 2) // HEAD_DIM
MLA_Q_RANK = 896
MLA_KV_RANK = 512
MLA_NOPE_DIM = 128
MLA_ROPE_DIM = 64
MLA_V_DIM = 128
MLA_WIDTH = N_MLA_HEADS * MLA_V_DIM
OD_IN_WIDTH = MLA_Q_RANK + MLA_KV_RANK + MLA_ROPE_DIM + POOL_WIDTH
OD_SPLITS = (MLA_Q_RANK, MLA_Q_RANK + MLA_KV_RANK, MLA_Q_RANK + MLA_KV_RANK + MLA_ROPE_DIM)
D_FF = -(-8 * D_MODEL // (3 * 256)) * 256
ADA_RANK = 256
N_MOD = 6
Q_BLOCK = 128
ROPE_BASE = 10000.0
EPS = 1e-6
N_EVEN = (DEPTH + 1) // 2
N_ODD = DEPTH // 2

kernel_name = 'hybrid_dit_shortconv_diffattn_pool_mla'

F32 = jnp.float32


def _rms_norm(x, g):
    xf = x.astype(F32)
    y = xf * lax.rsqrt(jnp.mean(xf * xf, axis=-1, keepdims=True) + EPS)
    return (y * g.astype(F32)).astype(x.dtype)


def _modulate(h, shift, scale):
    return h * (1 + scale) + shift


def _swiglu(h, w_gate, w_up, w_down):
    return (jax.nn.silu(h @ w_gate) * (h @ w_up)) @ w_down


def _axial_rope_tables(rows, d_rot):
    n = rows * GRID_W
    row = jnp.broadcast_to(jnp.arange(rows, dtype=F32)[:, None], (rows, GRID_W)).reshape(n)
    col = jnp.broadcast_to(jnp.arange(GRID_W, dtype=F32)[None, :], (rows, GRID_W)).reshape(n)
    n_freq = d_rot // 4
    inv_freq = ROPE_BASE ** (-jnp.arange(n_freq, dtype=F32) / n_freq)
    ang = jnp.concatenate([row[:, None] * inv_freq, col[:, None] * inv_freq], axis=-1)
    return jnp.cos(ang), jnp.sin(ang)


def _apply_rope(x, cos, sin):
    half = x.shape[-1] // 2
    shape = (1, x.shape[1]) + (1,) * (x.ndim - 3) + (half,)
    cos = cos.reshape(shape).astype(x.dtype)
    sin = sin.reshape(shape).astype(x.dtype)
    x1, x2 = x[..., :half], x[..., half:]
    return jnp.concatenate([x1 * cos - x2 * sin, x2 * cos + x1 * sin], axis=-1)


def _sweep_query_blocks(attend, *qs):
    b, n = qs[0].shape[:2]
    nb = n // Q_BLOCK
    blocks = tuple(jnp.swapaxes(q.reshape((b, nb, Q_BLOCK) + q.shape[2:]), 0, 1) for q in qs)
    out = lax.map(lambda qb: attend(*qb), blocks)
    return jnp.swapaxes(out, 0, 1).reshape((b, n) + out.shape[3:])


def _short_conv(u, w):
    t = u.shape[1]
    up = jnp.pad(u, ((0, 0), (1, 1), (0, 0)))
    return w[0] * up[:, :t] + w[1] * up[:, 1:t + 1] + w[2] * up[:, 2:]


def _diff_project(q, k, v, rope):
    b, t = q.shape[:2]
    q = q.reshape(b, t, N_DIFF_HEADS, 2 * DIFF_QK_DIM)
    k = k.reshape(b, t, N_DIFF_HEADS, 2 * DIFF_QK_DIM)
    v = v.reshape(b, t, N_DIFF_HEADS, DIFF_V_DIM)
    q1, q2 = q[..., :DIFF_QK_DIM], q[..., DIFF_QK_DIM:]
    k1, k2 = k[..., :DIFF_QK_DIM], k[..., DIFF_QK_DIM:]
    if rope is not None:
        cos, sin = rope
        q1, q2 = _apply_rope(q1, cos, sin), _apply_rope(q2, cos, sin)
        k1, k2 = _apply_rope(k1, cos, sin), _apply_rope(k2, cos, sin)
    return q1, q2, k1, k2, v


def _diff_attend(k1, k2, v, lam):
    scale = DIFF_QK_DIM ** -0.5

    def attend(q1, q2):
        s1 = jnp.einsum('bqhd,bkhd->bhqk', q1, k1).astype(F32) * scale
        s2 = jnp.einsum('bqhd,bkhd->bhqk', q2, k2).astype(F32) * scale
        a = jax.nn.softmax(s1, axis=-1) - lam * jax.nn.softmax(s2, axis=-1)
        return jnp.einsum('bhqk,bkhd->bqhd', a.astype(v.dtype), v)

    return attend


def _even_mixer(h, hc, w_in, conv_w, lam_q1, lam_k1, lam_q2, lam_k2, subln, w_out, lam_init, rope, need_ctx):
    b, n, _ = h.shape
    bg, cg, xa, q, k, v = jnp.split(h @ w_in, EV_SPLITS, axis=-1)
    bgc, cgc, xac, qc, kc, vc = jnp.split(hc @ w_in, EV_SPLITS, axis=-1)
    ya = bg * _short_conv(cg * xa, conv_w)
    q1, q2, k1, k2, v = _diff_project(q, k, v, rope)
    qc1, qc2, kc1, kc2, vc = _diff_project(qc, kc, vc, None)
    lam = (jnp.exp(jnp.sum(lam_q1.astype(F32) * lam_k1.astype(F32)))
           - jnp.exp(jnp.sum(lam_q2.astype(F32) * lam_k2.astype(F32))) + lam_init)
    attend = _diff_attend(jnp.concatenate([kc1, k1], axis=1), jnp.concatenate([kc2, k2], axis=1),
                          jnp.concatenate([vc, v], axis=1), lam)
    yb = _sweep_query_blocks(attend, q1, q2)
    yb = (_rms_norm(yb, subln) * (1 - lam_init)).reshape(b, n, DIFF_WIDTH)
    y = jnp.concatenate([ya, yb], axis=-1) @ w_out
    if not need_ctx:
        return y, None
    yac = bgc * _short_conv(cgc * xac, conv_w)
    ybc = _diff_attend(kc1, kc2, vc, lam)(qc1, qc2)
    ybc = (_rms_norm(ybc, subln) * (1 - lam_init)).reshape(b, -1, DIFF_WIDTH)
    yc = jnp.concatenate([yac, ybc], axis=-1) @ w_out
    return y, yc


def _multiscale_pool(u, pool_w, pool_scale):
    b, t, _ = u.shape
    ug = u.reshape(b, t, N_POOL_GROUPS, POOL_GROUP)
    cs = jnp.cumsum(ug.astype(F32), axis=1)
    cs0 = jnp.concatenate([jnp.zeros((b, 1, N_POOL_GROUPS, POOL_GROUP), F32), cs], axis=1)
    pos = jnp.arange(t)
    pooled = []
    for g, w in enumerate(POOL_WINDOWS):
        lo = jnp.clip(pos - w // 2, 0, t - 1)
        hi = jnp.clip(pos + w // 2 - 1, 0, t - 1)
        cnt = (hi - lo + 1).astype(F32)[None, :, None]
        pooled.append((cs0[:, hi + 1, g] - cs0[:, lo, g]) / cnt)
    pooled = jnp.stack(pooled, axis=2)
    diff = (pooled - ug.astype(F32)).astype(u.dtype)
    y = jnp.einsum('btgc,gce->btge', diff, pool_w).reshape(b, t, POOL_WIDTH)
    return y * pool_scale


def _mla_project(p, q_norm, w_uq, kv_norm, w_ukv, rope):
    b, t = p.shape[:2]
    cq, ckv, kpe, u = jnp.split(p, OD_SPLITS, axis=-1)
    q = (_rms_norm(cq, q_norm) @ w_uq).reshape(b, t, N_MLA_HEADS, MLA_NOPE_DIM + MLA_ROPE_DIM)
    kv = (_rms_norm(ckv, kv_norm) @ w_ukv).reshape(b, t, N_MLA_HEADS, MLA_NOPE_DIM + MLA_V_DIM)
    qn, qp = q[..., :MLA_NOPE_DIM], q[..., MLA_NOPE_DIM:]
    kn, v = kv[..., :MLA_NOPE_DIM], kv[..., MLA_NOPE_DIM:]
    if rope is not None:
        cos, sin = rope
        qp = _apply_rope(qp, cos, sin)
        kpe = _apply_rope(kpe, cos, sin)
    return qn, qp, kn, kpe, v, u


def _mla_attend(k_nope, k_pe, v):
    scale = (MLA_NOPE_DIM + MLA_ROPE_DIM) ** -0.5

    def attend(q_nope, q_pe):
        s = (jnp.einsum('bqhd,bkhd->bhqk', q_nope, k_nope)
             + jnp.einsum('bqhr,bkr->bhqk', q_pe, k_pe)).astype(F32) * scale
        p = jax.nn.softmax(s, axis=-1)
        return jnp.einsum('bhqk,bkhd->bqhd', p.astype(v.dtype), v)

    return attend


def _odd_mixer(h, hc, w_in, q_norm, w_uq, kv_norm, w_ukv, pool_w, pool_scale, w_out, rope, need_ctx):
    b, n, _ = h.shape
    qn, qp, kn, kp, v, u = _mla_project(h @ w_in, q_norm, w_uq, kv_norm, w_ukv, rope)
    qnc, qpc, knc, kpc, vc, uc = _mla_project(hc @ w_in, q_norm, w_uq, kv_norm, w_ukv, None)
    attend = _mla_attend(jnp.concatenate([knc, kn], axis=1), jnp.concatenate([kpc, kp], axis=1),
                         jnp.concatenate([vc, v], axis=1))
    ym = _sweep_query_blocks(attend, qn, qp).reshape(b, n, MLA_WIDTH)
    yp = _multiscale_pool(u, pool_w, pool_scale)
    y = jnp.concatenate([yp, ym], axis=-1) @ w_out
    if not need_ctx:
        return y, None
    ymc = _mla_attend(knc, kpc, vc)(qnc, qpc).reshape(b, -1, MLA_WIDTH)
    ypc = _multiscale_pool(uc, pool_w, pool_scale)
    yc = jnp.concatenate([ypc, ymc], axis=-1) @ w_out
    return y, yc


def setup_inputs(seed: int = 0) -> dict:
    key = jax.random.key(seed)
    d = D_MODEL
    specs = [
        ('x', (BATCH, SEQ, d), 1.0, 0.0),
        ('c', (BATCH, d), 1.0, 0.0),
        ('ctx', (BATCH, CTX_LEN, d), 1.0, 0.0),
        ('c_ctx', (d,), 1.0, 0.0),
        ('ada_down', (DEPTH, d, ADA_RANK), d ** -0.5, 0.0),
        ('ada_up', (DEPTH, ADA_RANK, N_MOD * d), 0.5 * ADA_RANK ** -0.5, 0.0),
        ('ada_bias', (DEPTH, N_MOD * d), 0.1, 0.0),
        ('norm_mix', (DEPTH, d), 0.05, 1.0),
        ('norm_ffn', (DEPTH, d), 0.05, 1.0),
        ('ffn_gate', (DEPTH, d, D_FF), d ** -0.5, 0.0),
        ('ffn_up', (DEPTH, d, D_FF), d ** -0.5, 0.0),
        ('ffn_down', (DEPTH, D_FF, d), D_FF ** -0.5, 0.0),
        ('ev_w_in', (N_EVEN, d, EV_IN_WIDTH), d ** -0.5, 0.0),
        ('ev_conv', (N_EVEN, CONV_WIDTH, A_WIDTH), CONV_WIDTH ** -0.5, 0.0),
        ('ev_lam_q1', (N_EVEN, DIFF_QK_DIM), 0.1, 0.0),
        ('ev_lam_k1', (N_EVEN, DIFF_QK_DIM), 0.1, 0.0),
        ('ev_lam_q2', (N_EVEN, DIFF_QK_DIM), 0.1, 0.0),
        ('ev_lam_k2', (N_EVEN, DIFF_QK_DIM), 0.1, 0.0),
        ('ev_subln', (N_EVEN, DIFF_V_DIM), 0.05, 1.0),
        ('ev_w_out', (N_EVEN, A_WIDTH + DIFF_WIDTH, d), (A_WIDTH + DIFF_WIDTH) ** -0.5, 0.0),
        ('od_w_in', (N_ODD, d, OD_IN_WIDTH), d ** -0.5, 0.0),
        ('od_q_norm', (N_ODD, MLA_Q_RANK), 0.05, 1.0),
        ('od_w_uq', (N_ODD, MLA_Q_RANK, N_MLA_HEADS * (MLA_NOPE_DIM + MLA_ROPE_DIM)), MLA_Q_RANK ** -0.5, 0.0),
        ('od_kv_norm', (N_ODD, MLA_KV_RANK), 0.05, 1.0),
        ('od_w_ukv', (N_ODD, MLA_KV_RANK, N_MLA_HEADS * (MLA_NOPE_DIM + MLA_V_DIM)), MLA_KV_RANK ** -0.5, 0.0),
        ('od_pool_w', (N_ODD, N_POOL_GROUPS, POOL_GROUP, POOL_GROUP), POOL_GROUP ** -0.5, 0.0),
        ('od_pool_scale', (N_ODD, POOL_WIDTH), 0.1, 1.0),
        ('od_w_out', (N_ODD, POOL_WIDTH + MLA_WIDTH, d), (POOL_WIDTH + MLA_WIDTH) ** -0.5, 0.0),
        ('final_norm', (d,), 0.05, 1.0),
    ]
    keys = jax.random.split(key, len(specs))
    return {name: offset + scale * jax.random.normal(keys[i], shape, jnp.float32)
            for i, (name, shape, scale, offset) in enumerate(specs)}


def reference(x, c, ctx, c_ctx, ada_down, ada_up, ada_bias, norm_mix, norm_ffn, ffn_gate, ffn_up, ffn_down,
              ev_w_in, ev_conv, ev_lam_q1, ev_lam_k1, ev_lam_q2, ev_lam_k2, ev_subln, ev_w_out,
              od_w_in, od_q_norm, od_w_uq, od_kv_norm, od_w_ukv, od_pool_w, od_pool_scale, od_w_out, final_norm):
    rows = x.shape[1] // GRID_W
    rope_diff = _axial_rope_tables(rows, DIFF_QK_DIM)
    rope_mla = _axial_rope_tables(rows, MLA_ROPE_DIM)
    silu_c = jax.nn.silu(c)
    silu_cc = jax.nn.silu(c_ctx)
    for l in range(DEPTH):
        need_ctx = l < DEPTH - 1
        mod = (silu_c @ ada_down[l]) @ ada_up[l] + ada_bias[l]
        mod_c = (silu_cc @ ada_down[l]) @ ada_up[l] + ada_bias[l]
        sh1, sc1, g1, sh2, sc2, g2 = jnp.split(mod[:, None, :], N_MOD, axis=-1)
        shc1, scc1, gc1, shc2, scc2, gc2 = jnp.split(mod_c, N_MOD, axis=-1)
        h = _modulate(_rms_norm(x, norm_mix[l]), sh1, sc1)
        hc = _modulate(_rms_norm(ctx, norm_mix[l]), shc1, scc1)
        if l % 2 == 0:
            i = l // 2
            lam_init = 0.8 - 0.6 * math.exp(-0.3 * l)
            y, yc = _even_mixer(h, hc, ev_w_in[i], ev_conv[i], ev_lam_q1[i], ev_lam_k1[i], ev_lam_q2[i],
                                ev_lam_k2[i], ev_subln[i], ev_w_out[i], lam_init, rope_diff, need_ctx)
        else:
            i = l // 2
            y, yc = _odd_mixer(h, hc, od_w_in[i], od_q_norm[i], od_w_uq[i], od_kv_norm[i], od_w_ukv[i],
                               od_pool_w[i], od_pool_scale[i], od_w_out[i], rope_mla, need_ctx)
        x = x + g1 * y
        x = x + g2 * _swiglu(_modulate(_rms_norm(x, norm_ffn[l]), sh2, sc2), ffn_gate[l], ffn_up[l], ffn_down[l])
        if need_ctx:
            ctx = ctx + gc1 * yc
            ctx = ctx + gc2 * _swiglu(_modulate(_rms_norm(ctx, norm_ffn[l]), shc2, scc2),
                                      ffn_gate[l], ffn_up[l], ffn_down[l])
    return _rms_norm(x, final_norm)
```

```python
import functools
import math

import jax
import jax.numpy as jnp
from jax import lax
from jax.experimental import pallas as pl
from jax.experimental.pallas import tpu as pltpu

F32 = jnp.float32
BF16 = jnp.bfloat16

LANES = 128
SUBLANES = 8
HEAD_DIM = 128
GRID_W = 64
ROPE_BASE = 10000.0
EPS = 1e-6
POOL_WINDOWS = (2, 4, 8, 16)
POOL_HALO = 8
LOG2E = 1.4426950408889634
VMEM_LIMIT = 56 * 1024 * 1024


def _pick(n, cands):
    for c in cands:
        if c <= n and n % c == 0:
            return c
    return n


def _params(*sem):
    return pltpu.CompilerParams(dimension_semantics=sem, vmem_limit_bytes=VMEM_LIMIT)


def _is_ctx_rows(tile, bm, n_lat):
    rows = tile * bm + lax.broadcasted_iota(jnp.int32, (bm, 1), 0)
    return rows >= n_lat


def _sel(is_ctx, ref):
    return jnp.where(is_ctx, ref[1:2, :], ref[0:1, :])


def _ada_kernel(cc_ref, down_ref, up_ref, bias_ref, o_ref, t_ref):
    @pl.when(pl.program_id(1) == 0)
    def _():
        c = cc_ref[...]
        s = c * jax.nn.sigmoid(c)
        t = jnp.dot(s.astype(BF16), down_ref[...], preferred_element_type=F32)
        t_ref[...] = t.astype(BF16)

    o_ref[...] = jnp.dot(t_ref[...], up_ref[...], preferred_element_type=F32) + bias_ref[...]


def _ada_call(cc, down, up, bias):
    depth, d, rank = down.shape
    n = up.shape[-1]
    bn = _pick(n, (2048, 1024, 512, 256, 128))
    return pl.pallas_call(
        _ada_kernel,
        grid=(depth, n // bn),
        in_specs=[
            pl.BlockSpec((SUBLANES, d), lambda l, j: (0, 0)),
            pl.BlockSpec((None, d, rank), lambda l, j: (l, 0, 0)),
            pl.BlockSpec((None, rank, bn), lambda l, j: (l, 0, j)),
            pl.BlockSpec((None, 1, bn), lambda l, j: (l, 0, j)),
        ],
        out_specs=pl.BlockSpec((None, SUBLANES, bn), lambda l, j: (l, 0, j)),
        out_shape=jax.ShapeDtypeStruct((depth, SUBLANES, n), F32),
        scratch_shapes=[pltpu.VMEM((SUBLANES, rank), BF16)],
        compiler_params=_params("arbitrary", "arbitrary"),
        name="ada_mod",
    )(cc, down, up, bias)


def _norm_kernel(*refs, modulate, bm, n_lat):
    if modulate:
        x_ref, g_ref, sh_ref, sc_ref, o_ref = refs
    else:
        x_ref, g_ref, o_ref = refs
    x = x_ref[...].astype(F32)
    y = x * lax.rsqrt(jnp.mean(x * x, axis=-1, keepdims=True) + EPS)
    y = y * g_ref[...]
    if modulate:
        is_ctx = _is_ctx_rows(pl.program_id(0), bm, n_lat)
        y = y * (1 + _sel(is_ctx, sc_ref)) + _sel(is_ctx, sh_ref)
    o_ref[...] = y.astype(o_ref.dtype)


def _norm_call(x, g, *, rows, out_dtype, mod=None, shift_blk=0, scale_blk=0, n_lat=0, name="rmsnorm"):
    k = x.shape[-1]
    bm = _pick(rows, (256, 128, 64, 32, 16, 8))
    in_specs = [pl.BlockSpec((bm, k), lambda i: (i, 0)), pl.BlockSpec((1, k), lambda i: (0, 0))]
    args = [x, g.reshape(1, k)]
    if mod is not None:
        in_specs += [pl.BlockSpec((SUBLANES, k), lambda i: (0, shift_blk)),
                     pl.BlockSpec((SUBLANES, k), lambda i: (0, scale_blk))]
        args += [mod, mod]
    return pl.pallas_call(
        functools.partial(_norm_kernel, modulate=mod is not None, bm=bm, n_lat=n_lat),
        grid=(rows // bm,),
        in_specs=in_specs,
        out_specs=pl.BlockSpec((bm, k), lambda i: (i, 0)),
        out_shape=jax.ShapeDtypeStruct((rows, k), out_dtype),
        compiler_params=_params("arbitrary"),
        name=name,
    )(*args)


def _rope(x, c_ref, s1_ref, s2_ref):
    c, s1, s2 = c_ref[...], s1_ref[...], s2_ref[...]
    out = []
    for b in range(x.shape[-1] // LANES):
        xb = x[:, b * LANES:(b + 1) * LANES]
        up = pltpu.roll(xb, LANES - 32, 1)
        dn = pltpu.roll(xb, 32, 1)
        out.append(xb * c + up * s1 + dn * s2)
    return jnp.concatenate(out, axis=1)


def _mm_kernel(*refs, rope_tiles):
    if rope_tiles is None:
        a_ref, w_ref, o_ref = refs
        o_ref[...] = jnp.dot(a_ref[...], w_ref[...], preferred_element_type=F32).astype(o_ref.dtype)
        return
    a_ref, w_ref, c_ref, s1_ref, s2_ref, o_ref = refs
    acc = jnp.dot(a_ref[...], w_ref[...], preferred_element_type=F32)
    j = pl.program_id(1)
    roped = (j >= rope_tiles[0]) & (j < rope_tiles[1])

    @pl.when(roped)
    def _():
        o_ref[...] = _rope(acc, c_ref, s1_ref, s2_ref).astype(o_ref.dtype)

    @pl.when(jnp.logical_not(roped))
    def _():
        o_ref[...] = acc.astype(o_ref.dtype)


def _mm_call(a, w, *, out_dtype, rope=None, rope_cols=None, name="mm"):
    m, k = a.shape
    n = w.shape[-1]
    bm = _pick(m, (768, 640, 512, 384, 256, 128))
    bn = _pick(n, (512, 896, 256, 128))
    in_specs = [pl.BlockSpec((bm, k), lambda i, j: (i, 0)), pl.BlockSpec((k, bn), lambda i, j: (0, j))]
    args = [a, w]
    rope_tiles = None
    if rope is not None:
        rope_tiles = (rope_cols[0] // bn, rope_cols[1] // bn)
        in_specs += [pl.BlockSpec((bm, LANES), lambda i, j: (i, 0))] * 3
        args += list(rope)
    return pl.pallas_call(
        functools.partial(_mm_kernel, rope_tiles=rope_tiles),
        grid=(m // bm, n // bn),
        in_specs=in_specs,
        out_specs=pl.BlockSpec((bm, bn), lambda i, j: (i, j)),
        out_shape=jax.ShapeDtypeStruct((m, n), out_dtype),
        compiler_params=_params("arbitrary", "arbitrary"),
        name=name,
    )(*args)


def _gu_kernel(a_ref, wg_ref, wu_ref, o_ref):
    a = a_ref[...]
    g = jnp.dot(a, wg_ref[...], preferred_element_type=F32)
    u = jnp.dot(a, wu_ref[...], preferred_element_type=F32)
    o_ref[...] = (g * jax.nn.sigmoid(g) * u).astype(o_ref.dtype)


def _gu_call(a, wg, wu):
    m, k = a.shape
    n = wg.shape[-1]
    bm = _pick(m, (768, 640, 512, 384, 256, 128))
    bn = _pick(n, (256, 128))
    return pl.pallas_call(
        _gu_kernel,
        grid=(m // bm, n // bn),
        in_specs=[pl.BlockSpec((bm, k), lambda i, j: (i, 0)),
                  pl.BlockSpec((k, bn), lambda i, j: (0, j)),
                  pl.BlockSpec((k, bn), lambda i, j: (0, j))],
        out_specs=pl.BlockSpec((bm, bn), lambda i, j: (i, j)),
        out_shape=jax.ShapeDtypeStruct((m, n), BF16),
        compiler_params=_params("arbitrary", "arbitrary"),
        name="ffn_gate_up",
    )(a, wg, wu)


def _mm_res_kernel(*refs, two, bm, n_lat):
    if two:
        a_ref, a2_ref, w_ref, w2_ref, res_ref, gate_ref, o_ref = refs
    else:
        a_ref, w_ref, res_ref, gate_ref, o_ref = refs
    acc = jnp.dot(a_ref[...], w_ref[...], preferred_element_type=F32)
    if two:
        acc = acc + jnp.dot(a2_ref[...], w2_ref[...], preferred_element_type=F32)
    gate = _sel(_is_ctx_rows(pl.program_id(0), bm, n_lat), gate_ref)
    o_ref[...] = res_ref[...] + gate * acc


def _mm_res_call(a_list, w_list, res, mod, gate_blk, *, n_lat, bm_cands, name):
    m, n = res.shape
    two = len(a_list) == 2
    bm = _pick(m, bm_cands)
    bn = _pick(n, (512, 256, 128))
    in_specs = [pl.BlockSpec((bm, a.shape[-1]), lambda i, j: (i, 0)) for a in a_list]
    in_specs += [pl.BlockSpec((w.shape[0], bn), lambda i, j: (0, j)) for w in w_list]
    in_specs += [pl.BlockSpec((bm, bn), lambda i, j: (i, j)),
                 pl.BlockSpec((SUBLANES, bn), lambda i, j: (0, gate_blk * (n // bn) + j))]
    return pl.pallas_call(
        functools.partial(_mm_res_kernel, two=two, bm=bm, n_lat=n_lat),
        grid=(m // bm, n // bn),
        in_specs=in_specs,
        out_specs=pl.BlockSpec((bm, bn), lambda i, j: (i, j)),
        out_shape=jax.ShapeDtypeStruct((m, n), F32),
        compiler_params=_params("arbitrary", "arbitrary"),
        name=name,
    )(*a_list, *w_list, res, mod)


def _conv_kernel(bg_ref, cg_ref, xa_ref, cgp_ref, xap_ref, cgn_ref, xan_ref, w_ref, o_ref, *, bm, n_lat, m):
    i = pl.program_id(0)
    u = cg_ref[...] * xa_ref[...]
    u_prev = cgp_ref[SUBLANES - 1:SUBLANES, :] * xap_ref[SUBLANES - 1:SUBLANES, :]
    u_next = cgn_ref[0:1, :] * xan_ref[0:1, :]
    loc = lax.broadcasted_iota(jnp.int32, (bm, 1), 0)
    rows = i * bm + loc
    prev = jnp.where(loc == 0, u_prev, pltpu.roll(u, 1, 0))
    prev = jnp.where((rows == 0) | (rows == n_lat), 0.0, prev)
    nxt = jnp.where(loc == bm - 1, u_next, pltpu.roll(u, bm - 1, 0))
    nxt = jnp.where((rows == n_lat - 1) | (rows == m - 1), 0.0, nxt)
    y = w_ref[0:1, :] * prev + w_ref[1:2, :] * u + w_ref[2:3, :] * nxt
    o_ref[...] = (bg_ref[...] * y).astype(o_ref.dtype)


def _conv_call(pa, conv_w, *, n_lat):
    m = pa.shape[0]
    aw = conv_w.shape[-1]
    bm = _pick(math.gcd(n_lat, m - n_lat), (256, 128, 64, 32, 16, 8))
    bc = _pick(aw, (512, 256, 128))
    nc = aw // bc
    rb = bm // SUBLANES
    last = m // SUBLANES - 1
    main = lambda off: pl.BlockSpec((bm, bc), lambda i, c: (i, off * nc + c))
    prev = lambda off: pl.BlockSpec((SUBLANES, bc), lambda i, c: (jnp.maximum(i * rb - 1, 0), off * nc + c))
    nxt = lambda off: pl.BlockSpec((SUBLANES, bc), lambda i, c: (jnp.minimum((i + 1) * rb, last), off * nc + c))
    return pl.pallas_call(
        functools.partial(_conv_kernel, bm=bm, n_lat=n_lat, m=m),
        grid=(m // bm, nc),
        in_specs=[main(0), main(1), main(2), prev(1), prev(2), nxt(1), nxt(2),
                  pl.BlockSpec((conv_w.shape[0], bc), lambda i, c: (0, c))],
        out_specs=pl.BlockSpec((bm, bc), lambda i, c: (i, c)),
        out_shape=jax.ShapeDtypeStruct((m, aw), BF16),
        compiler_params=_params("arbitrary", "arbitrary"),
        name="short_conv",
    )(pa, pa, pa, pa, pa, pa, pa, conv_w)


def _pool_kernel(u_ref, up_ref, un_ref, w_ref, sc_ref, o_ref, *, bm, n_lat, m, pg):
    i = pl.program_id(0)
    ext = bm + 2 * POOL_HALO
    is_ctx = i * bm >= n_lat
    lo = jnp.where(is_ctx, n_lat, 0)
    hi = jnp.where(is_ctx, m, n_lat)
    rows_ext = i * bm - POOL_HALO + lax.broadcasted_iota(jnp.int32, (ext, 1), 0)
    valid = (rows_ext >= lo) & (rows_ext < hi)
    t = i * bm + lax.broadcasted_iota(jnp.int32, (bm, 1), 0) - lo
    t_len = hi - lo
    for g, win in enumerate(POOL_WINDOWS):
        cols = slice(g * pg, (g + 1) * pg)
        u = u_ref[:, cols]
        e = jnp.concatenate([up_ref[:, cols], u, un_ref[:, cols]], axis=0)
        e = jnp.where(valid, e, 0.0)
        span = 1
        while span < win:
            e = e + pltpu.roll(e, ext - span, 0)
            span *= 2
        start = POOL_HALO - win // 2
        if start:
            e = pltpu.roll(e, ext - start, 0)
        wsum = e[0:bm, :]
        w_lo = jnp.clip(t - win // 2, 0, t_len - 1)
        w_hi = jnp.clip(t + win // 2 - 1, 0, t_len - 1)
        cnt = (w_hi - w_lo + 1).astype(F32)
        diff = (wsum / cnt - u).astype(BF16)
        y = jnp.dot(diff, w_ref[g], preferred_element_type=F32) * sc_ref[:, cols]
        o_ref[:, cols] = y.astype(o_ref.dtype)


def _pool_call(u, pool_w, pool_scale, *, n_lat):
    m, pw = u.shape
    pg = pw // len(POOL_WINDOWS)
    bm = _pick(math.gcd(n_lat, m - n_lat), (256, 128, 64, 32, 16, 8))
    rb = bm // SUBLANES
    last = m // SUBLANES - 1
    return pl.pallas_call(
        functools.partial(_pool_kernel, bm=bm, n_lat=n_lat, m=m, pg=pg),
        grid=(m // bm,),
        in_specs=[pl.BlockSpec((bm, pw), lambda i: (i, 0)),
                  pl.BlockSpec((SUBLANES, pw), lambda i: (jnp.maximum(i * rb - 1, 0), 0)),
                  pl.BlockSpec((SUBLANES, pw), lambda i: (jnp.minimum((i + 1) * rb, last), 0)),
                  pl.BlockSpec(pool_w.shape, lambda i: (0, 0, 0)),
                  pl.BlockSpec((1, pw), lambda i: (0, 0))],
        out_specs=pl.BlockSpec((bm, pw), lambda i: (i, 0)),
        out_shape=jax.ShapeDtypeStruct((m, pw), BF16),
        compiler_params=_params("arbitrary"),
        name="multiscale_pool",
    )(u, u, u, pool_w, pool_scale.reshape(1, pw))


def _flash_sweep(qs, k_chunk, v_ref, *, nk, bk, c):
    r = qs.shape[0]

    def body(t, carry):
        m_run, l_run, acc = carry
        k = k_chunk(t)
        v = v_ref[pl.ds(pl.multiple_of(t * bk, bk), bk), :]
        s = lax.dot_general(qs, k, (((1,), (1,)), ((), ())), preferred_element_type=F32)
        m_new = jnp.maximum(m_run, jnp.max(s, axis=1, keepdims=True))
        alpha = jnp.exp2((m_run - m_new) * c)
        p = jnp.exp2((s - m_new) * c)
        l_new = alpha * l_run + jnp.sum(p, axis=1, keepdims=True)
        acc = alpha * acc + jnp.dot(p.astype(BF16), v, preferred_element_type=F32)
        return m_new, l_new, acc

    init = (jnp.full((r, 1), -jnp.inf, F32), jnp.zeros((r, 1), F32), jnp.zeros((r, v_ref.shape[-1]), F32))
    _, l_run, acc = lax.fori_loop(0, nk, body, init)
    return l_run, acc


def _diff_attn_kernel(q_ref, k_ref, v_ref, lq1_ref, lk1_ref, lq2_ref, lk2_ref, g_ref, o_ref, *, bq, bk, nk, lam_init):
    q = q_ref[...]
    lane = lax.broadcasted_iota(jnp.int32, q.shape, 1)
    zero = jnp.zeros_like(q)
    qs = jnp.concatenate([jnp.where(lane < HEAD_DIM // 2, q, zero), jnp.where(lane >= HEAD_DIM // 2, q, zero)], axis=0)
    c = (HEAD_DIM // 2) ** -0.5 * LOG2E
    l_run, acc = _flash_sweep(qs, lambda t: k_ref[pl.ds(pl.multiple_of(t * bk, bk), bk), :], v_ref, nk=nk, bk=bk, c=c)
    o1 = acc[:bq] / l_run[:bq]
    o2 = acc[bq:] / l_run[bq:]
    lam = (jnp.exp(jnp.sum(lq1_ref[...] * lk1_ref[...], axis=1, keepdims=True))
           - jnp.exp(jnp.sum(lq2_ref[...] * lk2_ref[...], axis=1, keepdims=True)) + lam_init)
    o = o1 - lam * o2
    y = o * lax.rsqrt(jnp.mean(o * o, axis=-1, keepdims=True) + EPS)
    o_ref[...] = (y * g_ref[...] * (1 - lam_init)).astype(o_ref.dtype)


def _diff_attn_call(qkv, lam_vecs, subln, *, heads, q_rows, q_row0, kv_rows, kv_row0, lam_init, name):
    bq = _pick(q_rows, (256, 128))
    bk = _pick(kv_rows, (768, 512, 640, 384, 256, 128))
    qb0, kb0 = q_row0 // bq, kv_row0 // kv_rows
    vec = pl.BlockSpec((1, HEAD_DIM // 2), lambda h, i: (0, 0))
    return pl.pallas_call(
        functools.partial(_diff_attn_kernel, bq=bq, bk=bk, nk=kv_rows // bk, lam_init=lam_init),
        grid=(heads, q_rows // bq),
        in_specs=[pl.BlockSpec((bq, HEAD_DIM), lambda h, i: (qb0 + i, h)),
                  pl.BlockSpec((kv_rows, HEAD_DIM), lambda h, i: (kb0, heads + h)),
                  pl.BlockSpec((kv_rows, HEAD_DIM), lambda h, i: (kb0, 2 * heads + h)),
                  vec, vec, vec, vec,
                  pl.BlockSpec((1, HEAD_DIM), lambda h, i: (0, 0))],
        out_specs=pl.BlockSpec((bq, HEAD_DIM), lambda h, i: (i, h)),
        out_shape=jax.ShapeDtypeStruct((q_rows, heads * HEAD_DIM), BF16),
        compiler_params=_params("arbitrary", "arbitrary"),
        name=name,
    )(qkv, qkv, qkv, *lam_vecs, subln)


def _mla_attn_kernel(qn_ref, qp_ref, kn_ref, kp_ref, v_ref, o_ref, *, bk, nk, scale):
    qs = jnp.concatenate([qn_ref[...], qp_ref[...]], axis=1)

    def k_chunk(t):
        rows = pl.ds(pl.multiple_of(t * bk, bk), bk)
        return jnp.concatenate([kn_ref[rows, :], kp_ref[rows, :]], axis=1)

    l_run, acc = _flash_sweep(qs, k_chunk, v_ref, nk=nk, bk=bk, c=scale * LOG2E)
    o_ref[...] = (acc / l_run).astype(o_ref.dtype)


def _mla_attn_call(q, kv, kpe, *, heads, q_rows, q_row0, kv_rows, kv_row0, scale, name):
    bq = _pick(q_rows, (512, 256, 128))
    bk = _pick(kv_rows, (768, 512, 640, 384, 256, 128))
    qb0, kb0 = q_row0 // bq, kv_row0 // kv_rows
    return pl.pallas_call(
        functools.partial(_mla_attn_kernel, bk=bk, nk=kv_rows // bk, scale=scale),
        grid=(heads, q_rows // bq),
        in_specs=[pl.BlockSpec((bq, HEAD_DIM), lambda h, i: (qb0 + i, h)),
                  pl.BlockSpec((bq, HEAD_DIM), lambda h, i: (qb0 + i, heads + h)),
                  pl.BlockSpec((kv_rows, HEAD_DIM), lambda h, i: (kb0, h)),
                  pl.BlockSpec((kv_rows, LANES), lambda h, i: (kb0, 0)),
                  pl.BlockSpec((kv_rows, HEAD_DIM), lambda h, i: (kb0, heads + h))],
        out_specs=pl.BlockSpec((bq, HEAD_DIM), lambda h, i: (i, h)),
        out_shape=jax.ShapeDtypeStruct((q_rows, heads * HEAD_DIM), BF16),
        compiler_params=_params("arbitrary", "arbitrary"),
        name=name,
    )(q, q, kv, kpe, kv)


def _rope_tables(seq, ctx_len):
    rows = seq // GRID_W
    row = jnp.broadcast_to(jnp.arange(rows, dtype=F32)[:, None], (rows, GRID_W)).reshape(seq)
    col = jnp.broadcast_to(jnp.arange(GRID_W, dtype=F32)[None, :], (rows, GRID_W)).reshape(seq)
    n_freq = 16
    inv_freq = ROPE_BASE ** (-jnp.arange(n_freq, dtype=F32) / n_freq)
    ang = jnp.concatenate([row[:, None] * inv_freq, col[:, None] * inv_freq], axis=-1)
    cos, sin = jnp.cos(ang), jnp.sin(ang)
    zero = jnp.zeros_like(sin)
    c = jnp.concatenate([cos, cos, cos, cos], axis=-1)
    s1 = jnp.concatenate([-sin, zero, -sin, zero], axis=-1)
    s2 = jnp.concatenate([zero, sin, zero, sin], axis=-1)
    pad = lambda t, v: jnp.concatenate([t, jnp.full((ctx_len, LANES), v, F32)], axis=0)
    return pad(c, 1.0), pad(s1, 0.0), pad(s2, 0.0)


def kernel(x, c, ctx, c_ctx, ada_down, ada_up, ada_bias, norm_mix, norm_ffn, ffn_gate, ffn_up, ffn_down, ev_w_in, ev_conv, ev_lam_q1, ev_lam_k1, ev_lam_q2, ev_lam_k2, ev_subln, ev_w_out, od_w_in, od_q_norm, od_w_uq, od_kv_norm, od_w_ukv, od_pool_w, od_pool_scale, od_w_out, final_norm):
    _, seq, d = x.shape
    ctx_len = ctx.shape[1]
    m = seq + ctx_len
    depth = ada_down.shape[0]
    half = d // 2
    heads = half // HEAD_DIM
    q_rank = od_q_norm.shape[-1]
    kv_rank = od_kv_norm.shape[-1]
    rope_dim = HEAD_DIM // 2
    bf = lambda t: t.astype(BF16)

    r = jnp.concatenate([x[0], ctx[0]], axis=0)
    tables = _rope_tables(seq, ctx_len)

    cc = jnp.zeros((SUBLANES, d), F32).at[0].set(c[0]).at[1].set(c_ctx)
    mods = _ada_call(cc, bf(ada_down), bf(ada_up), ada_bias.reshape(depth, 1, -1))

    row_tiles = (768, 640, 512, 384, 256, 128)
    for l in range(depth):
        mod = mods[l]
        i = l // 2
        h = _norm_call(r, norm_mix[l], rows=m, out_dtype=BF16, mod=mod, shift_blk=0, scale_blk=1, n_lat=seq,
                       name="norm_mod_mix")
        if l % 2 == 0:
            lam_init = 0.8 - 0.6 * math.exp(-0.3 * l)
            w_in = ev_w_in[i]
            pa = _mm_call(h, bf(w_in[:, :3 * half]), out_dtype=F32, name="ev_in_conv")
            qkv = _mm_call(h, bf(w_in[:, 3 * half:]), out_dtype=BF16, rope=tables, rope_cols=(0, 2 * half),
                           name="ev_in_qkv")
            ya = _conv_call(pa, ev_conv[i], n_lat=seq)
            lam_vecs = [v[i].reshape(1, -1) for v in (ev_lam_q1, ev_lam_k1, ev_lam_q2, ev_lam_k2)]
            subln = ev_subln[i].reshape(1, -1)
            yb_lat = _diff_attn_call(qkv, lam_vecs, subln, heads=heads, q_rows=seq, q_row0=0, kv_rows=m, kv_row0=0,
                                     lam_init=lam_init, name="diff_attn")
            yb_ctx = _diff_attn_call(qkv, lam_vecs, subln, heads=heads, q_rows=ctx_len, q_row0=seq,
                                     kv_rows=ctx_len, kv_row0=seq, lam_init=lam_init, name="diff_attn_ctx")
            y1, y2 = ya, jnp.concatenate([yb_lat, yb_ctx], axis=0)
            w_out = ev_w_out[i]
        else:
            w_in = od_w_in[i]
            w_kpe = jnp.pad(w_in[:, q_rank + kv_rank:q_rank + kv_rank + rope_dim], ((0, 0), (0, LANES - rope_dim)))
            cq = _mm_call(h, bf(w_in[:, :q_rank]), out_dtype=F32, name="od_in_cq")
            ckv = _mm_call(h, bf(w_in[:, q_rank:q_rank + kv_rank]), out_dtype=F32, name="od_in_ckv")
            kpe = _mm_call(h, bf(w_kpe), out_dtype=BF16, rope=tables, rope_cols=(0, LANES), name="od_in_kpe")
            u = _mm_call(h, bf(w_in[:, q_rank + kv_rank + rope_dim:]), out_dtype=F32, name="od_in_pool")
            cqn = _norm_call(cq, od_q_norm[i], rows=m, out_dtype=BF16, name="norm_cq")
            ckvn = _norm_call(ckv, od_kv_norm[i], rows=m, out_dtype=BF16, name="norm_ckv")
            w_uq = od_w_uq[i].reshape(q_rank, heads, HEAD_DIM + rope_dim)
            w_qn = w_uq[:, :, :HEAD_DIM].reshape(q_rank, half)
            w_qp = jnp.pad(w_uq[:, :, HEAD_DIM:], ((0, 0), (0, 0), (0, LANES - rope_dim))).reshape(q_rank, heads * LANES)
            q = _mm_call(cqn, bf(jnp.concatenate([w_qn, w_qp], axis=1)), out_dtype=BF16, rope=tables,
                         rope_cols=(half, half + heads * LANES), name="od_uq")
            w_ukv = od_w_ukv[i].reshape(kv_rank, heads, 2 * HEAD_DIM)
            w_kv = jnp.concatenate([w_ukv[:, :, :HEAD_DIM].reshape(kv_rank, half),
                                    w_ukv[:, :, HEAD_DIM:].reshape(kv_rank, half)], axis=1)
            kv = _mm_call(ckvn, bf(w_kv), out_dtype=BF16, name="od_ukv")
            scale = (HEAD_DIM + rope_dim) ** -0.5
            ym_lat = _mla_attn_call(q, kv, kpe, heads=heads, q_rows=seq, q_row0=0, kv_rows=m, kv_row0=0,
                                    scale=scale, name="mla_attn")
            ym_ctx = _mla_attn_call(q, kv, kpe, heads=heads, q_rows=ctx_len, q_row0=seq, kv_rows=ctx_len,
                                    kv_row0=seq, scale=scale, name="mla_attn_ctx")
            y1 = _pool_call(u, bf(od_pool_w[i]), od_pool_scale[i], n_lat=seq)
            y2 = jnp.concatenate([ym_lat, ym_ctx], axis=0)
            w_out = od_w_out[i]
        r = _mm_res_call([y1, y2], [bf(w_out[:half]), bf(w_out[half:])], r, mod, 2, n_lat=seq,
                         bm_cands=row_tiles, name="mix_out")
        h2 = _norm_call(r, norm_ffn[l], rows=m, out_dtype=BF16, mod=mod, shift_blk=3, scale_blk=4, n_lat=seq,
                        name="norm_mod_ffn")
        a = _gu_call(h2, bf(ffn_gate[l]), bf(ffn_up[l]))
        r = _mm_res_call([a], [bf(ffn_down[l])], r, mod, 5, n_lat=seq, bm_cands=(384, 320, 256, 128),
                         name="ffn_down")
    out = _norm_call(r, final_norm, rows=seq, out_dtype=F32, name="final_norm")
    return out[None]
```

```python
import functools
import math

import jax
import jax.numpy as jnp
from jax import lax
from jax.experimental import pallas as pl
from jax.experimental.pallas import tpu as pltpu

F32 = jnp.float32
BF16 = jnp.bfloat16

LANES = 128
SUBLANES = 8
HEAD_DIM = 128
GRID_W = 64
ROPE_BASE = 10000.0
EPS = 1e-6
POOL_WINDOWS = (2, 4, 8, 16)
POOL_HALO = 8
LOG2E = 1.4426950408889634
VMEM_LIMIT = 56 * 1024 * 1024


def _pick(n, cands):
    for c in cands:
        if c <= n and n % c == 0:
            return c
    return n


def _params(*sem):
    return pltpu.CompilerParams(dimension_semantics=sem, vmem_limit_bytes=VMEM_LIMIT)


def _is_ctx_rows(tile, bm, n_lat):
    rows = tile * bm + lax.broadcasted_iota(jnp.int32, (bm, 1), 0)
    return rows >= n_lat


def _sel(is_ctx, ref):
    return jnp.where(is_ctx, ref[1:2, :], ref[0:1, :])


def _ada_kernel(cc_ref, down_ref, up_ref, bias_ref, o_ref, t_ref):
    @pl.when(pl.program_id(1) == 0)
    def _():
        c = cc_ref[...]
        s = c * jax.nn.sigmoid(c)
        t = jnp.dot(s.astype(BF16), down_ref[...], preferred_element_type=F32)
        t_ref[...] = t.astype(BF16)

    o_ref[...] = jnp.dot(t_ref[...], up_ref[...], preferred_element_type=F32) + bias_ref[...]


def _ada_call(cc, down, up, bias):
    depth, d, rank = down.shape
    n = up.shape[-1]
    bn = _pick(n, (2048, 1024, 512, 256, 128))
    return pl.pallas_call(
        _ada_kernel,
        grid=(depth, n // bn),
        in_specs=[
            pl.BlockSpec((SUBLANES, d), lambda l, j: (0, 0)),
            pl.BlockSpec((None, d, rank), lambda l, j: (l, 0, 0)),
            pl.BlockSpec((None, rank, bn), lambda l, j: (l, 0, j)),
            pl.BlockSpec((None, 1, bn), lambda l, j: (l, 0, j)),
        ],
        out_specs=pl.BlockSpec((None, SUBLANES, bn), lambda l, j: (l, 0, j)),
        out_shape=jax.ShapeDtypeStruct((depth, SUBLANES, n), F32),
        scratch_shapes=[pltpu.VMEM((SUBLANES, rank), BF16)],
        compiler_params=_params("arbitrary", "arbitrary"),
        name="ada_mod",
    )(cc, down, up, bias)


def _norm_kernel(*refs, modulate, bm, n_lat):
    if modulate:
        x_ref, g_ref, sh_ref, sc_ref, o_ref = refs
    else:
        x_ref, g_ref, o_ref = refs
    x = x_ref[...].astype(F32)
    y = x * lax.rsqrt(jnp.mean(x * x, axis=-1, keepdims=True) + EPS)
    y = y * g_ref[...]
    if modulate:
        is_ctx = _is_ctx_rows(pl.program_id(0), bm, n_lat)
        y = y * (1 + _sel(is_ctx, sc_ref)) + _sel(is_ctx, sh_ref)
    o_ref[...] = y.astype(o_ref.dtype)


def _norm_call(x, g, *, rows, out_dtype, mod=None, shift_blk=0, scale_blk=0, n_lat=0, name="rmsnorm"):
    k = x.shape[-1]
    bm = _pick(rows, (256, 128, 64, 32, 16, 8))
    in_specs = [pl.BlockSpec((bm, k), lambda i: (i, 0)), pl.BlockSpec((1, k), lambda i: (0, 0))]
    args = [x, g.reshape(1, k)]
    if mod is not None:
        in_specs += [pl.BlockSpec((SUBLANES, k), lambda i: (0, shift_blk)),
                     pl.BlockSpec((SUBLANES, k), lambda i: (0, scale_blk))]
        args += [mod, mod]
    return pl.pallas_call(
        functools.partial(_norm_kernel, modulate=mod is not None, bm=bm, n_lat=n_lat),
        grid=(rows // bm,),
        in_specs=in_specs,
        out_specs=pl.BlockSpec((bm, k), lambda i: (i, 0)),
        out_shape=jax.ShapeDtypeStruct((rows, k), out_dtype),
        compiler_params=_params("arbitrary"),
        name=name,
    )(*args)


def _rope(x, c_ref, s1_ref, s2_ref):
    c, s1, s2 = c_ref[...], s1_ref[...], s2_ref[...]
    out = []
    for b in range(x.shape[-1] // LANES):
        xb = x[:, b * LANES:(b + 1) * LANES]
        up = pltpu.roll(xb, LANES - 32, 1)
        dn = pltpu.roll(xb, 32, 1)
        out.append(xb * c + up * s1 + dn * s2)
    return jnp.concatenate(out, axis=1)


def _mm_kernel(*refs, rope_tiles):
    if rope_tiles is None:
        a_ref, w_ref, o_ref = refs
        o_ref[...] = jnp.dot(a_ref[...], w_ref[...], preferred_element_type=F32).astype(o_ref.dtype)
        return
    a_ref, w_ref, c_ref, s1_ref, s2_ref, o_ref = refs
    acc = jnp.dot(a_ref[...], w_ref[...], preferred_element_type=F32)
    j = pl.program_id(1)
    roped = (j >= rope_tiles[0]) & (j < rope_tiles[1])

    @pl.when(roped)
    def _():
        o_ref[...] = _rope(acc, c_ref, s1_ref, s2_ref).astype(o_ref.dtype)

    @pl.when(jnp.logical_not(roped))
    def _():
        o_ref[...] = acc.astype(o_ref.dtype)


def _mm_call(a, w, *, out_dtype, rope=None, rope_cols=None, name="mm"):
    m, k = a.shape
    n = w.shape[-1]
    bm = _pick(m, (768, 640, 512, 384, 256, 128))
    bn = _pick(n, (512, 896, 256, 128))
    in_specs = [pl.BlockSpec((bm, k), lambda i, j: (i, 0)), pl.BlockSpec((k, bn), lambda i, j: (0, j))]
    args = [a, w]
    rope_tiles = None
    if rope is not None:
        rope_tiles = (rope_cols[0] // bn, rope_cols[1] // bn)
        in_specs += [pl.BlockSpec((bm, LANES), lambda i, j: (i, 0))] * 3
        args += list(rope)
    return pl.pallas_call(
        functools.partial(_mm_kernel, rope_tiles=rope_tiles),
        grid=(m // bm, n // bn),
        in_specs=in_specs,
        out_specs=pl.BlockSpec((bm, bn), lambda i, j: (i, j)),
        out_shape=jax.ShapeDtypeStruct((m, n), out_dtype),
        compiler_params=_params("arbitrary", "arbitrary"),
        name=name,
    )(*args)


def _mm_res_kernel(a_ref, w_ref, res_ref, gate_ref, o_ref, *, bm, n_lat):
    acc = jnp.dot(a_ref[...], w_ref[...], preferred_element_type=F32)
    gate = _sel(_is_ctx_rows(pl.program_id(0), bm, n_lat), gate_ref)
    o_ref[...] = res_ref[...] + gate * acc


def _mm_res_call(a, w, res, mod, gate_blk, *, n_lat, name):
    m, n = res.shape
    k = a.shape[-1]
    bm = _pick(m, (384, 320, 256, 128))
    bn = _pick(n, (512, 256, 128))
    return pl.pallas_call(
        functools.partial(_mm_res_kernel, bm=bm, n_lat=n_lat),
        grid=(m // bm, n // bn),
        in_specs=[pl.BlockSpec((bm, k), lambda i, j: (i, 0)),
                  pl.BlockSpec((k, bn), lambda i, j: (0, j)),
                  pl.BlockSpec((bm, bn), lambda i, j: (i, j)),
                  pl.BlockSpec((SUBLANES, bn), lambda i, j: (0, gate_blk * (n // bn) + j))],
        out_specs=pl.BlockSpec((bm, bn), lambda i, j: (i, j)),
        out_shape=jax.ShapeDtypeStruct((m, n), F32),
        compiler_params=_params("arbitrary", "arbitrary"),
        name=name,
    )(a, w, res, mod)


CAST_ROWS = 256


def _cast_weights(w_refs, wb_refs):
    for w_ref, wb_ref in zip(w_refs, wb_refs):
        def step(r, _):
            rows = pl.ds(pl.multiple_of(r * CAST_ROWS, CAST_ROWS), CAST_ROWS)
            wb_ref[rows, :] = w_ref[rows, :].astype(BF16)
            return 0
        lax.fori_loop(0, w_ref.shape[0] // CAST_ROWS, step, 0)


def _wmm_kernel(*refs, parts, epi, bm, n_lat, rope_tiles):
    n_extra = {"plain": 0, "rope": 3, "res": 2, "swiglu": 0}[epi]
    n_w = 2 * parts if epi == "swiglu" else parts
    a_refs, w_refs = refs[:parts], refs[parts:parts + n_w]
    extra = refs[parts + n_w:parts + n_w + n_extra]
    o_ref = refs[parts + n_w + n_extra]
    wb_refs = refs[parts + n_w + n_extra + 1:]
    j, i = pl.program_id(0), pl.program_id(1)

    @pl.when(i == 0)
    def _():
        _cast_weights(w_refs, wb_refs)

    def mm(wbs):
        acc = jnp.dot(a_refs[0][...], wbs[0][...], preferred_element_type=F32)
        for a_ref, wb_ref in zip(a_refs[1:], wbs[1:]):
            acc = acc + jnp.dot(a_ref[...], wb_ref[...], preferred_element_type=F32)
        return acc

    if epi == "swiglu":
        g = mm(wb_refs[:parts])
        u = mm(wb_refs[parts:])
        o_ref[...] = (g * jax.nn.sigmoid(g) * u).astype(o_ref.dtype)
    elif epi == "res":
        res_ref, gate_ref = extra
        gate = _sel(_is_ctx_rows(i, bm, n_lat), gate_ref)
        o_ref[...] = res_ref[...] + gate * mm(wb_refs)
    elif epi == "rope":
        acc = mm(wb_refs)
        roped = (j >= rope_tiles[0]) & (j < rope_tiles[1])

        @pl.when(roped)
        def _():
            o_ref[...] = _rope(acc, *extra).astype(o_ref.dtype)

        @pl.when(jnp.logical_not(roped))
        def _():
            o_ref[...] = acc.astype(o_ref.dtype)
    else:
        o_ref[...] = mm(wb_refs).astype(o_ref.dtype)


def _wmm_call(a_list, weights, *, n, col0=0, out_dtype, epi="plain", rope=None, rope_cols=None, res=None, mod=None,
              gate_blk=0, n_lat=0, bn_cands=(512, 256, 128), name="wmm"):
    m = a_list[0].shape[0]
    parts = len(a_list)
    bm = _pick(m, (768, 640, 512, 384, 256, 128))
    bn = _pick(math.gcd(n, col0) if col0 else n, bn_cands)
    cb0 = col0 // bn
    in_specs = [pl.BlockSpec((bm, a.shape[1]), lambda j, i: (i, 0)) for a in a_list]
    args = list(a_list)
    scratch = []
    for idx, (w, layer, rb) in enumerate(weights):
        k = a_list[idx % parts].shape[1]
        if layer is None:
            in_specs.append(pl.BlockSpec((k, bn), lambda j, i, rb=rb: (rb, cb0 + j)))
        else:
            in_specs.append(pl.BlockSpec((None, k, bn), lambda j, i, layer=layer, rb=rb: (layer, rb, cb0 + j)))
        args.append(w)
        scratch.append(pltpu.VMEM((k, bn), BF16))
    rope_tiles = None
    if epi == "rope":
        rope_tiles = (rope_cols[0] // bn, rope_cols[1] // bn)
        in_specs += [pl.BlockSpec((bm, LANES), lambda j, i: (i, 0))] * 3
        args += list(rope)
    elif epi == "res":
        in_specs += [pl.BlockSpec((bm, bn), lambda j, i: (i, j)),
                     pl.BlockSpec((SUBLANES, bn), lambda j, i: (0, gate_blk * (n // bn) + j))]
        args += [res, mod]
    return pl.pallas_call(
        functools.partial(_wmm_kernel, parts=parts, epi=epi, bm=bm, n_lat=n_lat, rope_tiles=rope_tiles),
        grid=(n // bn, m // bm),
        in_specs=in_specs,
        out_specs=pl.BlockSpec((bm, bn), lambda j, i: (i, j)),
        out_shape=jax.ShapeDtypeStruct((m, n), out_dtype),
        scratch_shapes=scratch,
        compiler_params=_params("arbitrary", "arbitrary"),
        name=name,
    )(*args)


def _conv_kernel(bg_ref, cg_ref, xa_ref, cgp_ref, xap_ref, cgn_ref, xan_ref, w_ref, o_ref, *, bm, n_lat, m):
    i = pl.program_id(0)
    u = cg_ref[...] * xa_ref[...]
    u_prev = cgp_ref[SUBLANES - 1:SUBLANES, :] * xap_ref[SUBLANES - 1:SUBLANES, :]
    u_next = cgn_ref[0:1, :] * xan_ref[0:1, :]
    loc = lax.broadcasted_iota(jnp.int32, (bm, 1), 0)
    rows = i * bm + loc
    prev = jnp.where(loc == 0, u_prev, pltpu.roll(u, 1, 0))
    prev = jnp.where((rows == 0) | (rows == n_lat), 0.0, prev)
    nxt = jnp.where(loc == bm - 1, u_next, pltpu.roll(u, bm - 1, 0))
    nxt = jnp.where((rows == n_lat - 1) | (rows == m - 1), 0.0, nxt)
    y = w_ref[0:1, :] * prev + w_ref[1:2, :] * u + w_ref[2:3, :] * nxt
    o_ref[...] = (bg_ref[...] * y).astype(o_ref.dtype)


def _conv_call(pa, conv_w, *, n_lat):
    m = pa.shape[0]
    aw = conv_w.shape[-1]
    bm = _pick(math.gcd(n_lat, m - n_lat), (256, 128, 64, 32, 16, 8))
    bc = _pick(aw, (512, 256, 128))
    nc = aw // bc
    rb = bm // SUBLANES
    last = m // SUBLANES - 1
    main = lambda off: pl.BlockSpec((bm, bc), lambda i, c: (i, off * nc + c))
    prev = lambda off: pl.BlockSpec((SUBLANES, bc), lambda i, c: (jnp.maximum(i * rb - 1, 0), off * nc + c))
    nxt = lambda off: pl.BlockSpec((SUBLANES, bc), lambda i, c: (jnp.minimum((i + 1) * rb, last), off * nc + c))
    return pl.pallas_call(
        functools.partial(_conv_kernel, bm=bm, n_lat=n_lat, m=m),
        grid=(m // bm, nc),
        in_specs=[main(0), main(1), main(2), prev(1), prev(2), nxt(1), nxt(2),
                  pl.BlockSpec((conv_w.shape[0], bc), lambda i, c: (0, c))],
        out_specs=pl.BlockSpec((bm, bc), lambda i, c: (i, c)),
        out_shape=jax.ShapeDtypeStruct((m, aw), BF16),
        compiler_params=_params("arbitrary", "arbitrary"),
        name="short_conv",
    )(pa, pa, pa, pa, pa, pa, pa, conv_w)


def _pool_kernel(u_ref, up_ref, un_ref, w_ref, sc_ref, o_ref, *, bm, n_lat, m, pg):
    i = pl.program_id(0)
    ext = bm + 2 * POOL_HALO
    is_ctx = i * bm >= n_lat
    lo = jnp.where(is_ctx, n_lat, 0)
    hi = jnp.where(is_ctx, m, n_lat)
    rows_ext = i * bm - POOL_HALO + lax.broadcasted_iota(jnp.int32, (ext, 1), 0)
    valid = (rows_ext >= lo) & (rows_ext < hi)
    t = i * bm + lax.broadcasted_iota(jnp.int32, (bm, 1), 0) - lo
    t_len = hi - lo
    for g, win in enumerate(POOL_WINDOWS):
        cols = slice(g * pg, (g + 1) * pg)
        u = u_ref[:, cols]
        e = jnp.concatenate([up_ref[:, cols], u, un_ref[:, cols]], axis=0)
        e = jnp.where(valid, e, 0.0)
        span = 1
        while span < win:
            e = e + pltpu.roll(e, ext - span, 0)
            span *= 2
        start = POOL_HALO - win // 2
        if start:
            e = pltpu.roll(e, ext - start, 0)
        wsum = e[0:bm, :]
        w_lo = jnp.clip(t - win // 2, 0, t_len - 1)
        w_hi = jnp.clip(t + win // 2 - 1, 0, t_len - 1)
        cnt = (w_hi - w_lo + 1).astype(F32)
        diff = (wsum / cnt - u).astype(BF16)
        y = jnp.dot(diff, w_ref[g], preferred_element_type=F32) * sc_ref[:, cols]
        o_ref[:, cols] = y.astype(o_ref.dtype)


def _pool_call(u, pool_w, pool_scale, *, n_lat):
    m, pw = u.shape
    pg = pw // len(POOL_WINDOWS)
    bm = _pick(math.gcd(n_lat, m - n_lat), (256, 128, 64, 32, 16, 8))
    rb = bm // SUBLANES
    last = m // SUBLANES - 1
    return pl.pallas_call(
        functools.partial(_pool_kernel, bm=bm, n_lat=n_lat, m=m, pg=pg),
        grid=(m // bm,),
        in_specs=[pl.BlockSpec((bm, pw), lambda i: (i, 0)),
                  pl.BlockSpec((SUBLANES, pw), lambda i: (jnp.maximum(i * rb - 1, 0), 0)),
                  pl.BlockSpec((SUBLANES, pw), lambda i: (jnp.minimum((i + 1) * rb, last), 0)),
                  pl.BlockSpec(pool_w.shape, lambda i: (0, 0, 0)),
                  pl.BlockSpec((1, pw), lambda i: (0, 0))],
        out_specs=pl.BlockSpec((bm, pw), lambda i: (i, 0)),
        out_shape=jax.ShapeDtypeStruct((m, pw), BF16),
        compiler_params=_params("arbitrary"),
        name="multiscale_pool",
    )(u, u, u, pool_w, pool_scale.reshape(1, pw))


def _flash_sweep(qs, k_chunk, v_ref, *, nk, bk, c):
    chains = 2
    r = qs.shape[0]
    rc = r // chains
    q_parts = [qs[n * rc:(n + 1) * rc] for n in range(chains)]
    dv = v_ref.shape[-1]

    def scores_of(t):
        k = k_chunk(t)
        return tuple(lax.dot_general(q_parts[n], k, (((1,), (1,)), ((), ())), preferred_element_type=F32)
                     for n in range(chains))

    def update(t, scores, state):
        v = v_ref[t * bk:(t + 1) * bk, :]
        out = []
        for n in range(chains):
            m_run, l_run, acc = state[n]
            s = scores[n]
            m_new = jnp.maximum(m_run, jnp.max(s, axis=1, keepdims=True))
            alpha = jnp.exp2((m_run - m_new) * c)
            p = jnp.exp2((s - m_new) * c)
            l_new = alpha * l_run + jnp.sum(p, axis=1, keepdims=True)
            acc = alpha * acc + jnp.dot(p.astype(BF16), v, preferred_element_type=F32)
            out.append((m_new, l_new, acc))
        return tuple(out)

    state = tuple((jnp.full((rc, 1), -jnp.inf, F32), jnp.zeros((rc, 1), F32), jnp.zeros((rc, dv), F32))
                  for _ in range(chains))
    scores = scores_of(0)
    for t in range(nk - 1):
        nxt = scores_of(t + 1)
        state = update(t, scores, state)
        scores = nxt
    fin = update(nk - 1, scores, state)
    return (jnp.concatenate([f[1] for f in fin], axis=0), jnp.concatenate([f[2] for f in fin], axis=0))


def _diff_attn_kernel(q_ref, k_ref, v_ref, lq1_ref, lk1_ref, lq2_ref, lk2_ref, g_ref, o_ref, *, bq, bk, nk, lam_init):
    q = q_ref[...]
    lane = lax.broadcasted_iota(jnp.int32, q.shape, 1)
    zero = jnp.zeros_like(q)
    qs = jnp.concatenate([jnp.where(lane < HEAD_DIM // 2, q, zero), jnp.where(lane >= HEAD_DIM // 2, q, zero)], axis=0)
    c = (HEAD_DIM // 2) ** -0.5 * LOG2E
    l_run, acc = _flash_sweep(qs, lambda t: k_ref[t * bk:(t + 1) * bk, :], v_ref, nk=nk, bk=bk, c=c)
    o1 = acc[:bq] / l_run[:bq]
    o2 = acc[bq:] / l_run[bq:]
    lam = (jnp.exp(jnp.sum(lq1_ref[...] * lk1_ref[...], axis=1, keepdims=True))
           - jnp.exp(jnp.sum(lq2_ref[...] * lk2_ref[...], axis=1, keepdims=True)) + lam_init)
    o = o1 - lam * o2
    y = o * lax.rsqrt(jnp.mean(o * o, axis=-1, keepdims=True) + EPS)
    o_ref[...] = (y * g_ref[...] * (1 - lam_init)).astype(o_ref.dtype)


def _diff_attn_call(qkv, lam_vecs, subln, *, heads, q_rows, q_row0, kv_rows, kv_row0, lam_init, name):
    bq = _pick(q_rows, (256, 128))
    bk = _pick(kv_rows, (768, 512, 640, 384, 256, 128))
    qb0, kb0 = q_row0 // bq, kv_row0 // kv_rows
    vec = pl.BlockSpec((1, HEAD_DIM // 2), lambda h, i: (0, 0))
    return pl.pallas_call(
        functools.partial(_diff_attn_kernel, bq=bq, bk=bk, nk=kv_rows // bk, lam_init=lam_init),
        grid=(heads, q_rows // bq),
        in_specs=[pl.BlockSpec((bq, HEAD_DIM), lambda h, i: (qb0 + i, h)),
                  pl.BlockSpec((kv_rows, HEAD_DIM), lambda h, i: (kb0, heads + h)),
                  pl.BlockSpec((kv_rows, HEAD_DIM), lambda h, i: (kb0, 2 * heads + h)),
                  vec, vec, vec, vec,
                  pl.BlockSpec((1, HEAD_DIM), lambda h, i: (0, 0))],
        out_specs=pl.BlockSpec((bq, HEAD_DIM), lambda h, i: (i, h)),
        out_shape=jax.ShapeDtypeStruct((q_rows, heads * HEAD_DIM), BF16),
        compiler_params=_params("arbitrary", "arbitrary"),
        name=name,
    )(qkv, qkv, qkv, *lam_vecs, subln)


def _mla_attn_kernel(qn_ref, qp_ref, kn_ref, kp_ref, v_ref, o_ref, *, bk, nk, scale):
    qs = jnp.concatenate([qn_ref[...], qp_ref[...]], axis=1)

    def k_chunk(t):
        rows = slice(t * bk, (t + 1) * bk)
        return jnp.concatenate([kn_ref[rows, :], kp_ref[rows, :]], axis=1)

    l_run, acc = _flash_sweep(qs, k_chunk, v_ref, nk=nk, bk=bk, c=scale * LOG2E)
    o_ref[...] = (acc / l_run).astype(o_ref.dtype)


def _mla_attn_call(q, kv, kpe, *, heads, q_rows, q_row0, kv_rows, kv_row0, scale, name):
    bq = _pick(q_rows, (512, 256, 128))
    bk = _pick(kv_rows, (768, 512, 640, 384, 256, 128))
    qb0, kb0 = q_row0 // bq, kv_row0 // kv_rows
    return pl.pallas_call(
        functools.partial(_mla_attn_kernel, bk=bk, nk=kv_rows // bk, scale=scale),
        grid=(heads, q_rows // bq),
        in_specs=[pl.BlockSpec((bq, HEAD_DIM), lambda h, i: (qb0 + i, h)),
                  pl.BlockSpec((bq, HEAD_DIM), lambda h, i: (qb0 + i, heads + h)),
                  pl.BlockSpec((kv_rows, HEAD_DIM), lambda h, i: (kb0, h)),
                  pl.BlockSpec((kv_rows, LANES), lambda h, i: (kb0, 0)),
                  pl.BlockSpec((kv_rows, HEAD_DIM), lambda h, i: (kb0, heads + h))],
        out_specs=pl.BlockSpec((bq, HEAD_DIM), lambda h, i: (i, h)),
        out_shape=jax.ShapeDtypeStruct((q_rows, heads * HEAD_DIM), BF16),
        compiler_params=_params("arbitrary", "arbitrary"),
        name=name,
    )(q, q, kv, kpe, kv)


def _rope_tables(seq, ctx_len):
    rows = seq // GRID_W
    row = jnp.broadcast_to(jnp.arange(rows, dtype=F32)[:, None], (rows, GRID_W)).reshape(seq)
    col = jnp.broadcast_to(jnp.arange(GRID_W, dtype=F32)[None, :], (rows, GRID_W)).reshape(seq)
    n_freq = 16
    inv_freq = ROPE_BASE ** (-jnp.arange(n_freq, dtype=F32) / n_freq)
    ang = jnp.concatenate([row[:, None] * inv_freq, col[:, None] * inv_freq], axis=-1)
    cos, sin = jnp.cos(ang), jnp.sin(ang)
    zero = jnp.zeros_like(sin)
    c = jnp.concatenate([cos, cos, cos, cos], axis=-1)
    s1 = jnp.concatenate([-sin, zero, -sin, zero], axis=-1)
    s2 = jnp.concatenate([zero, sin, zero, sin], axis=-1)
    pad = lambda t, v: jnp.concatenate([t, jnp.full((ctx_len, LANES), v, F32)], axis=0)
    return pad(c, 1.0), pad(s1, 0.0), pad(s2, 0.0)


def kernel(x, c, ctx, c_ctx, ada_down, ada_up, ada_bias, norm_mix, norm_ffn, ffn_gate, ffn_up, ffn_down, ev_w_in, ev_conv, ev_lam_q1, ev_lam_k1, ev_lam_q2, ev_lam_k2, ev_subln, ev_w_out, od_w_in, od_q_norm, od_w_uq, od_kv_norm, od_w_ukv, od_pool_w, od_pool_scale, od_w_out, final_norm):
    _, seq, d = x.shape
    ctx_len = ctx.shape[1]
    m = seq + ctx_len
    depth = ada_down.shape[0]
    half = d // 2
    heads = half // HEAD_DIM
    q_rank = od_q_norm.shape[-1]
    kv_rank = od_kv_norm.shape[-1]
    rope_dim = HEAD_DIM // 2
    bf = lambda t: t.astype(BF16)

    r = jnp.concatenate([x[0], ctx[0]], axis=0)
    tables = _rope_tables(seq, ctx_len)

    cc = jnp.zeros((SUBLANES, d), F32).at[0].set(c[0]).at[1].set(c_ctx)
    mods = _ada_call(cc, bf(ada_down), bf(ada_up), ada_bias.reshape(depth, 1, -1))

    for l in range(depth):
        mod = mods[l]
        i = l // 2
        h = _norm_call(r, norm_mix[l], rows=m, out_dtype=BF16, mod=mod, shift_blk=0, scale_blk=1, n_lat=seq,
                       name="norm_mod_mix")
        if l % 2 == 0:
            lam_init = 0.8 - 0.6 * math.exp(-0.3 * l)
            pa = _wmm_call([h], [(ev_w_in, i, 0)], n=3 * half, out_dtype=F32, name="ev_in_conv")
            qkv = _wmm_call([h], [(ev_w_in, i, 0)], n=3 * half, col0=3 * half, out_dtype=BF16, epi="rope",
                            rope=tables, rope_cols=(0, 2 * half), name="ev_in_qkv")
            ya = _conv_call(pa, ev_conv[i], n_lat=seq)
            lam_vecs = [v[i].reshape(1, -1) for v in (ev_lam_q1, ev_lam_k1, ev_lam_q2, ev_lam_k2)]
            subln = ev_subln[i].reshape(1, -1)
            yb_lat = _diff_attn_call(qkv, lam_vecs, subln, heads=heads, q_rows=seq, q_row0=0, kv_rows=m, kv_row0=0,
                                     lam_init=lam_init, name="diff_attn")
            yb_ctx = _diff_attn_call(qkv, lam_vecs, subln, heads=heads, q_rows=ctx_len, q_row0=seq,
                                     kv_rows=ctx_len, kv_row0=seq, lam_init=lam_init, name="diff_attn_ctx")
            y1, y2 = ya, jnp.concatenate([yb_lat, yb_ctx], axis=0)
            w_out = ev_w_out
        else:
            w_in = od_w_in[i]
            w_kpe = jnp.pad(w_in[:, q_rank + kv_rank:q_rank + kv_rank + rope_dim], ((0, 0), (0, LANES - rope_dim)))
            cq = _mm_call(h, bf(w_in[:, :q_rank]), out_dtype=F32, name="od_in_cq")
            ckv = _mm_call(h, bf(w_in[:, q_rank:q_rank + kv_rank]), out_dtype=F32, name="od_in_ckv")
            kpe = _mm_call(h, bf(w_kpe), out_dtype=BF16, rope=tables, rope_cols=(0, LANES), name="od_in_kpe")
            u = _mm_call(h, bf(w_in[:, q_rank + kv_rank + rope_dim:]), out_dtype=F32, name="od_in_pool")
            cqn = _norm_call(cq, od_q_norm[i], rows=m, out_dtype=BF16, name="norm_cq")
            ckvn = _norm_call(ckv, od_kv_norm[i], rows=m, out_dtype=BF16, name="norm_ckv")
            w_uq = od_w_uq[i].reshape(q_rank, heads, HEAD_DIM + rope_dim)
            w_qn = w_uq[:, :, :HEAD_DIM].reshape(q_rank, half)
            w_qp = jnp.pad(w_uq[:, :, HEAD_DIM:], ((0, 0), (0, 0), (0, LANES - rope_dim))).reshape(q_rank, heads * LANES)
            q = _mm_call(cqn, bf(jnp.concatenate([w_qn, w_qp], axis=1)), out_dtype=BF16, rope=tables,
                         rope_cols=(half, half + heads * LANES), name="od_uq")
            w_ukv = od_w_ukv[i].reshape(kv_rank, heads, 2 * HEAD_DIM)
            w_kv = jnp.concatenate([w_ukv[:, :, :HEAD_DIM].reshape(kv_rank, half),
                                    w_ukv[:, :, HEAD_DIM:].reshape(kv_rank, half)], axis=1)
            kv = _mm_call(ckvn, bf(w_kv), out_dtype=BF16, name="od_ukv")
            scale = (HEAD_DIM + rope_dim) ** -0.5
            ym_lat = _mla_attn_call(q, kv, kpe, heads=heads, q_rows=seq, q_row0=0, kv_rows=m, kv_row0=0,
                                    scale=scale, name="mla_attn")
            ym_ctx = _mla_attn_call(q, kv, kpe, heads=heads, q_rows=ctx_len, q_row0=seq, kv_rows=ctx_len,
                                    kv_row0=seq, scale=scale, name="mla_attn_ctx")
            y1 = _pool_call(u, bf(od_pool_w[i]), od_pool_scale[i], n_lat=seq)
            y2 = jnp.concatenate([ym_lat, ym_ctx], axis=0)
            w_out = od_w_out
        r = _wmm_call([y1, y2], [(w_out, i, 0), (w_out, i, 1)], n=d, out_dtype=F32, epi="res", res=r, mod=mod,
                      gate_blk=2, n_lat=seq, name="mix_out")
        h2 = _norm_call(r, norm_ffn[l], rows=m, out_dtype=BF16, mod=mod, shift_blk=3, scale_blk=4, n_lat=seq,
                        name="norm_mod_ffn")
        a = _wmm_call([h2], [(ffn_gate, l, 0), (ffn_up, l, 0)], n=ffn_gate.shape[-1], out_dtype=BF16, epi="swiglu",
                      bn_cands=(256, 128), name="ffn_gate_up")
        r = _mm_res_call(a, bf(ffn_down[l]), r, mod, 5, n_lat=seq, name="ffn_down")
    out = _norm_call(r, final_norm, rows=seq, out_dtype=F32, name="final_norm")
    return out[None]
```

```python
import functools
import math

import jax
import jax.numpy as jnp
from jax import lax
from jax.experimental import pallas as pl
from jax.experimental.pallas import tpu as pltpu

F32 = jnp.float32
BF16 = jnp.bfloat16

LANES = 128
SUBLANES = 8
HEAD_DIM = 128
GRID_W = 64
ROPE_BASE = 10000.0
EPS = 1e-6
POOL_WINDOWS = (2, 4, 8, 16)
POOL_HALO = 8
LOG2E = 1.4426950408889634
VMEM_LIMIT = 56 * 1024 * 1024


def _pick(n, cands):
    for c in cands:
        if c <= n and n % c == 0:
            return c
    return n


def _params(*sem):
    return pltpu.CompilerParams(dimension_semantics=sem, vmem_limit_bytes=VMEM_LIMIT)


def _is_ctx_rows(tile, bm, n_lat):
    rows = tile * bm + lax.broadcasted_iota(jnp.int32, (bm, 1), 0)
    return rows >= n_lat


def _sel(is_ctx, ref):
    return jnp.where(is_ctx, ref[1:2, :], ref[0:1, :])


def _ada_kernel(cc_ref, down_ref, up_ref, bias_ref, o_ref, t_ref):
    @pl.when(pl.program_id(1) == 0)
    def _():
        c = cc_ref[...]
        s = c * jax.nn.sigmoid(c)
        t = jnp.dot(s.astype(BF16), down_ref[...], preferred_element_type=F32)
        t_ref[...] = t.astype(BF16)

    o_ref[...] = jnp.dot(t_ref[...], up_ref[...], preferred_element_type=F32) + bias_ref[...]


def _ada_call(cc, down, up, bias):
    depth, d, rank = down.shape
    n = up.shape[-1]
    bn = _pick(n, (2048, 1024, 512, 256, 128))
    return pl.pallas_call(
        _ada_kernel,
        grid=(depth, n // bn),
        in_specs=[
            pl.BlockSpec((SUBLANES, d), lambda l, j: (0, 0)),
            pl.BlockSpec((None, d, rank), lambda l, j: (l, 0, 0)),
            pl.BlockSpec((None, rank, bn), lambda l, j: (l, 0, j)),
            pl.BlockSpec((None, 1, bn), lambda l, j: (l, 0, j)),
        ],
        out_specs=pl.BlockSpec((None, SUBLANES, bn), lambda l, j: (l, 0, j)),
        out_shape=jax.ShapeDtypeStruct((depth, SUBLANES, n), F32),
        scratch_shapes=[pltpu.VMEM((SUBLANES, rank), BF16)],
        compiler_params=_params("arbitrary", "arbitrary"),
        name="ada_mod",
    )(cc, down, up, bias)


NORM_ROWS = 16


def _norm_kernel(*refs, modulate, bm, n_lat):
    if modulate:
        x_ref, g_ref, sh_ref, sc_ref, o_ref = refs
    else:
        x_ref, g_ref, o_ref = refs
    g = g_ref[...]
    row0 = pl.program_id(0) * bm

    def step(r, _):
        rows = pl.ds(pl.multiple_of(r * NORM_ROWS, NORM_ROWS), NORM_ROWS)
        x = x_ref[rows, :].astype(F32)
        y = x * lax.rsqrt(jnp.mean(x * x, axis=-1, keepdims=True) + EPS)
        y = y * g
        if modulate:
            is_ctx = row0 + r * NORM_ROWS >= n_lat
            y = y * (1 + _sel(is_ctx, sc_ref)) + _sel(is_ctx, sh_ref)
        o_ref[rows, :] = y.astype(o_ref.dtype)
        return 0

    lax.fori_loop(0, bm // NORM_ROWS, step, 0, unroll=4)


def _norm_call(x, g, *, rows, out_dtype, mod=None, shift_blk=0, scale_blk=0, n_lat=0, name="rmsnorm"):
    k = x.shape[-1]
    bm = _pick(rows, (384, 320, 256, 128, 64, 32, 16, 8))
    in_specs = [pl.BlockSpec((bm, k), lambda i: (i, 0)), pl.BlockSpec((1, k), lambda i: (0, 0))]
    args = [x, g.reshape(1, k)]
    if mod is not None:
        in_specs += [pl.BlockSpec((SUBLANES, k), lambda i: (0, shift_blk)),
                     pl.BlockSpec((SUBLANES, k), lambda i: (0, scale_blk))]
        args += [mod, mod]
    return pl.pallas_call(
        functools.partial(_norm_kernel, modulate=mod is not None, bm=bm, n_lat=n_lat),
        grid=(rows // bm,),
        in_specs=in_specs,
        out_specs=pl.BlockSpec((bm, k), lambda i: (i, 0)),
        out_shape=jax.ShapeDtypeStruct((rows, k), out_dtype),
        compiler_params=_params("arbitrary"),
        name=name,
    )(*args)


def _rope(x, c_ref, s1_ref, s2_ref):
    c, s1, s2 = c_ref[...], s1_ref[...], s2_ref[...]
    out = []
    for b in range(x.shape[-1] // LANES):
        xb = x[:, b * LANES:(b + 1) * LANES]
        up = pltpu.roll(xb, LANES - 32, 1)
        dn = pltpu.roll(xb, 32, 1)
        out.append(xb * c + up * s1 + dn * s2)
    return jnp.concatenate(out, axis=1)


def _mm_kernel(*refs, rope_tiles, plain_scale):
    if rope_tiles is None:
        a_ref, w_ref, o_ref = refs
        o_ref[...] = jnp.dot(a_ref[...], w_ref[...], preferred_element_type=F32).astype(o_ref.dtype)
        return
    a_ref, w_ref, c_ref, s1_ref, s2_ref, o_ref = refs
    acc = jnp.dot(a_ref[...], w_ref[...], preferred_element_type=F32)
    j = pl.program_id(1)
    roped = (j >= rope_tiles[0]) & (j < rope_tiles[1])

    @pl.when(roped)
    def _():
        o_ref[...] = _rope(acc, c_ref, s1_ref, s2_ref).astype(o_ref.dtype)

    @pl.when(jnp.logical_not(roped))
    def _():
        o_ref[...] = (acc * plain_scale).astype(o_ref.dtype)


def _rope_table_specs(bm, bn, rope_q_cols, row_col):
    q0, q1 = (0, 0) if rope_q_cols is None else (rope_q_cols[0] // bn, rope_q_cols[1] // bn)

    def index(*grid):
        i, j = row_col(*grid)
        return i, jnp.where((j >= q0) & (j < q1), 1, 0)

    return [pl.BlockSpec((bm, LANES), index)] * 3


def _mm_call(a, w, *, out_dtype, rope=None, rope_cols=None, rope_q_cols=None, plain_scale=1.0, name="mm"):
    m, k = a.shape
    n = w.shape[-1]
    bm = _pick(m, (768, 640, 512, 384, 256, 128))
    bn = _pick(n, (512, 896, 256, 128))
    in_specs = [pl.BlockSpec((bm, k), lambda i, j: (i, 0)), pl.BlockSpec((k, bn), lambda i, j: (0, j))]
    args = [a, w]
    rope_tiles = None
    if rope is not None:
        rope_tiles = (rope_cols[0] // bn, rope_cols[1] // bn)
        in_specs += _rope_table_specs(bm, bn, rope_q_cols, lambda i, j: (i, j))
        args += list(rope)
    return pl.pallas_call(
        functools.partial(_mm_kernel, rope_tiles=rope_tiles, plain_scale=plain_scale),
        grid=(m // bm, n // bn),
        in_specs=in_specs,
        out_specs=pl.BlockSpec((bm, bn), lambda i, j: (i, j)),
        out_shape=jax.ShapeDtypeStruct((m, n), out_dtype),
        compiler_params=_params("arbitrary", "arbitrary"),
        name=name,
    )(*args)


def _nt_kernel(wt_ref, a_ref, o_ref):
    o_ref[...] = lax.dot_general(wt_ref[...], a_ref[...], (((1,), (1,)), ((), ())),
                                 preferred_element_type=F32).astype(o_ref.dtype)


def _nt_call(wt, a, *, name):
    c, k = wt.shape
    m = a.shape[0]
    bc = _pick(c, (512, 256, 128))
    bt = _pick(m, (768, 640, 512, 384, 256, 128))
    return pl.pallas_call(
        _nt_kernel,
        grid=(c // bc, m // bt),
        in_specs=[pl.BlockSpec((bc, k), lambda i, j: (i, 0)), pl.BlockSpec((bt, k), lambda i, j: (j, 0))],
        out_specs=pl.BlockSpec((bc, bt), lambda i, j: (i, j)),
        out_shape=jax.ShapeDtypeStruct((c, m), BF16),
        compiler_params=_params("arbitrary", "arbitrary"),
        name=name,
    )(wt, a)


def _mm_res_kernel(a_ref, w_ref, res_ref, gate_ref, o_ref, *, bm, n_lat):
    acc = jnp.dot(a_ref[...], w_ref[...], preferred_element_type=F32)
    gate = _sel(_is_ctx_rows(pl.program_id(0), bm, n_lat), gate_ref)
    o_ref[...] = res_ref[...] + gate * acc


def _mm_res_call(a, w, res, mod, gate_blk, *, n_lat, name):
    m, n = res.shape
    k = a.shape[-1]
    bm = _pick(m, (768, 640, 384, 256, 128))
    bn = _pick(n, (256, 128))
    return pl.pallas_call(
        functools.partial(_mm_res_kernel, bm=bm, n_lat=n_lat),
        grid=(m // bm, n // bn),
        in_specs=[pl.BlockSpec((bm, k), lambda i, j: (i, 0)),
                  pl.BlockSpec((k, bn), lambda i, j: (0, j)),
                  pl.BlockSpec((bm, bn), lambda i, j: (i, j)),
                  pl.BlockSpec((SUBLANES, bn), lambda i, j: (0, gate_blk * (n // bn) + j))],
        out_specs=pl.BlockSpec((bm, bn), lambda i, j: (i, j)),
        out_shape=jax.ShapeDtypeStruct((m, n), F32),
        compiler_params=_params("arbitrary", "arbitrary"),
        name=name,
    )(a, w, res, mod)


CAST_ROWS = 256


def _cast_weights(w_refs, wb_refs):
    for w_ref, wb_ref in zip(w_refs, wb_refs):
        def step(r, _):
            rows = pl.ds(pl.multiple_of(r * CAST_ROWS, CAST_ROWS), CAST_ROWS)
            wb_ref[rows, :] = w_ref[rows, :].astype(BF16)
            return 0
        lax.fori_loop(0, w_ref.shape[0] // CAST_ROWS, step, 0)


def _wmm_kernel(*refs, parts, epi, bm, n_lat, rope_tiles):
    n_extra = {"plain": 0, "rope": 3, "res": 2, "swiglu": 0}[epi]
    n_w = 2 * parts if epi == "swiglu" else parts
    a_refs, w_refs = refs[:parts], refs[parts:parts + n_w]
    extra = refs[parts + n_w:parts + n_w + n_extra]
    o_ref = refs[parts + n_w + n_extra]
    wb_refs = refs[parts + n_w + n_extra + 1:]
    j, i = pl.program_id(0), pl.program_id(1)

    @pl.when(i == 0)
    def _():
        _cast_weights(w_refs, wb_refs)

    def mm(wbs):
        acc = jnp.dot(a_refs[0][...], wbs[0][...], preferred_element_type=F32)
        for a_ref, wb_ref in zip(a_refs[1:], wbs[1:]):
            acc = acc + jnp.dot(a_ref[...], wb_ref[...], preferred_element_type=F32)
        return acc

    if epi == "swiglu":
        g = mm(wb_refs[:parts])
        u = mm(wb_refs[parts:])
        o_ref[...] = (g * jax.nn.sigmoid(g) * u).astype(o_ref.dtype)
    elif epi == "res":
        res_ref, gate_ref = extra
        gate = _sel(_is_ctx_rows(i, bm, n_lat), gate_ref)
        o_ref[...] = res_ref[...] + gate * mm(wb_refs)
    elif epi == "rope":
        acc = mm(wb_refs)
        roped = (j >= rope_tiles[0]) & (j < rope_tiles[1])

        @pl.when(roped)
        def _():
            o_ref[...] = _rope(acc, *extra).astype(o_ref.dtype)

        @pl.when(jnp.logical_not(roped))
        def _():
            o_ref[...] = acc.astype(o_ref.dtype)
    else:
        o_ref[...] = mm(wb_refs).astype(o_ref.dtype)


def _wmm_call(a_list, weights, *, n, col0=0, out_dtype, epi="plain", rope=None, rope_cols=None, rope_q_cols=None,
              res=None, mod=None,
              gate_blk=0, n_lat=0, bm_cands=(768, 640, 512, 384, 256, 128), bn_cands=(512, 256, 128), name="wmm"):
    m = a_list[0].shape[0]
    parts = len(a_list)
    bm = _pick(m, bm_cands)
    bn = _pick(math.gcd(n, col0) if col0 else n, bn_cands)
    cb0 = col0 // bn
    in_specs = [pl.BlockSpec((bm, a.shape[1]), lambda j, i: (i, 0)) for a in a_list]
    args = list(a_list)
    scratch = []
    for idx, (w, layer, rb) in enumerate(weights):
        k = a_list[idx % parts].shape[1]
        if layer is None:
            in_specs.append(pl.BlockSpec((k, bn), lambda j, i, rb=rb: (rb, cb0 + j)))
        else:
            in_specs.append(pl.BlockSpec((None, k, bn), lambda j, i, layer=layer, rb=rb: (layer, rb, cb0 + j)))
        args.append(w)
        scratch.append(pltpu.VMEM((k, bn), BF16))
    rope_tiles = None
    if epi == "rope":
        rope_tiles = (rope_cols[0] // bn, rope_cols[1] // bn)
        in_specs += _rope_table_specs(bm, bn, rope_q_cols, lambda j, i: (i, j))
        args += list(rope)
    elif epi == "res":
        in_specs += [pl.BlockSpec((bm, bn), lambda j, i: (i, j)),
                     pl.BlockSpec((SUBLANES, bn), lambda j, i: (0, gate_blk * (n // bn) + j))]
        args += [res, mod]
    return pl.pallas_call(
        functools.partial(_wmm_kernel, parts=parts, epi=epi, bm=bm, n_lat=n_lat, rope_tiles=rope_tiles),
        grid=(n // bn, m // bm),
        in_specs=in_specs,
        out_specs=pl.BlockSpec((bm, bn), lambda j, i: (i, j)),
        out_shape=jax.ShapeDtypeStruct((m, n), out_dtype),
        scratch_shapes=scratch,
        compiler_params=_params("arbitrary", "arbitrary"),
        name=name,
    )(*args)


def _conv_kernel(bg_ref, cg_ref, xa_ref, cgp_ref, xap_ref, cgn_ref, xan_ref, w_ref, o_ref, *, bm, n_lat, m):
    i = pl.program_id(0)
    u = cg_ref[...] * xa_ref[...]
    u_prev = cgp_ref[SUBLANES - 1:SUBLANES, :] * xap_ref[SUBLANES - 1:SUBLANES, :]
    u_next = cgn_ref[0:1, :] * xan_ref[0:1, :]
    loc = lax.broadcasted_iota(jnp.int32, (bm, 1), 0)
    rows = i * bm + loc
    prev = jnp.where(loc == 0, u_prev, pltpu.roll(u, 1, 0))
    prev = jnp.where((rows == 0) | (rows == n_lat), 0.0, prev)
    nxt = jnp.where(loc == bm - 1, u_next, pltpu.roll(u, bm - 1, 0))
    nxt = jnp.where((rows == n_lat - 1) | (rows == m - 1), 0.0, nxt)
    y = w_ref[0:1, :] * prev + w_ref[1:2, :] * u + w_ref[2:3, :] * nxt
    o_ref[...] = (bg_ref[...] * y).astype(o_ref.dtype)


def _conv_call(pa, conv_w, *, n_lat):
    m = pa.shape[0]
    aw = conv_w.shape[-1]
    bm = _pick(math.gcd(n_lat, m - n_lat), (256, 128, 64, 32, 16, 8))
    bc = _pick(aw, (512, 256, 128))
    nc = aw // bc
    rb = bm // SUBLANES
    last = m // SUBLANES - 1
    main = lambda off: pl.BlockSpec((bm, bc), lambda i, c: (i, off * nc + c))
    prev = lambda off: pl.BlockSpec((SUBLANES, bc), lambda i, c: (jnp.maximum(i * rb - 1, 0), off * nc + c))
    nxt = lambda off: pl.BlockSpec((SUBLANES, bc), lambda i, c: (jnp.minimum((i + 1) * rb, last), off * nc + c))
    return pl.pallas_call(
        functools.partial(_conv_kernel, bm=bm, n_lat=n_lat, m=m),
        grid=(m // bm, nc),
        in_specs=[main(0), main(1), main(2), prev(1), prev(2), nxt(1), nxt(2),
                  pl.BlockSpec((conv_w.shape[0], bc), lambda i, c: (0, c))],
        out_specs=pl.BlockSpec((bm, bc), lambda i, c: (i, c)),
        out_shape=jax.ShapeDtypeStruct((m, aw), BF16),
        compiler_params=_params("arbitrary", "arbitrary"),
        name="short_conv",
    )(pa, pa, pa, pa, pa, pa, pa, conv_w)


def _pool_kernel(u_ref, up_ref, un_ref, w_ref, sc_ref, o_ref, *, bm, n_lat, m, pg):
    i = pl.program_id(0)
    ext = bm + 2 * POOL_HALO
    is_ctx = i * bm >= n_lat
    lo = jnp.where(is_ctx, n_lat, 0)
    hi = jnp.where(is_ctx, m, n_lat)
    rows_ext = i * bm - POOL_HALO + lax.broadcasted_iota(jnp.int32, (ext, 1), 0)
    valid = (rows_ext >= lo) & (rows_ext < hi)
    t = i * bm + lax.broadcasted_iota(jnp.int32, (bm, 1), 0) - lo
    t_len = hi - lo
    for g, win in enumerate(POOL_WINDOWS):
        cols = slice(g * pg, (g + 1) * pg)
        u = u_ref[:, cols]
        e = jnp.concatenate([up_ref[:, cols], u, un_ref[:, cols]], axis=0)
        e = jnp.where(valid, e, 0.0)
        span = 1
        while span < win:
            e = e + pltpu.roll(e, ext - span, 0)
            span *= 2
        start = POOL_HALO - win // 2
        if start:
            e = pltpu.roll(e, ext - start, 0)
        wsum = e[0:bm, :]
        w_lo = jnp.clip(t - win // 2, 0, t_len - 1)
        w_hi = jnp.clip(t + win // 2 - 1, 0, t_len - 1)
        cnt = (w_hi - w_lo + 1).astype(F32)
        diff = (wsum / cnt - u).astype(BF16)
        y = jnp.dot(diff, w_ref[g], preferred_element_type=F32) * sc_ref[:, cols]
        o_ref[:, cols] = y.astype(o_ref.dtype)


def _pool_call(u, pool_w, pool_scale, *, n_lat):
    m, pw = u.shape
    pg = pw // len(POOL_WINDOWS)
    bm = _pick(math.gcd(n_lat, m - n_lat), (256, 128, 64, 32, 16, 8))
    rb = bm // SUBLANES
    last = m // SUBLANES - 1
    return pl.pallas_call(
        functools.partial(_pool_kernel, bm=bm, n_lat=n_lat, m=m, pg=pg),
        grid=(m // bm,),
        in_specs=[pl.BlockSpec((bm, pw), lambda i: (i, 0)),
                  pl.BlockSpec((SUBLANES, pw), lambda i: (jnp.maximum(i * rb - 1, 0), 0)),
                  pl.BlockSpec((SUBLANES, pw), lambda i: (jnp.minimum((i + 1) * rb, last), 0)),
                  pl.BlockSpec(pool_w.shape, lambda i: (0, 0, 0)),
                  pl.BlockSpec((1, pw), lambda i: (0, 0))],
        out_specs=pl.BlockSpec((bm, pw), lambda i: (i, 0)),
        out_shape=jax.ShapeDtypeStruct((m, pw), BF16),
        compiler_params=_params("arbitrary"),
        name="multiscale_pool",
    )(u, u, u, pool_w, pool_scale.reshape(1, pw))


def _flash_tiles(q_tiles, k_chunk, vt_ref, finish, *, nk, bk):
    dv = vt_ref.shape[0]

    def update(t, s, state):
        m_run, l_run, acc = state
        m_new = jnp.maximum(m_run, jnp.max(s, axis=0, keepdims=True))
        alpha = jnp.exp2(m_run - m_new)
        p = jnp.exp2(s - m_new)
        l_new = alpha * l_run + jnp.sum(p, axis=0, keepdims=True)
        acc = alpha * acc + jnp.dot(vt_ref[:, t * bk:(t + 1) * bk], p.astype(BF16), preferred_element_type=F32)
        return m_new, l_new, acc

    pending = None
    for n, qs in enumerate(q_tiles):
        r = qs.shape[0]
        scores_of = lambda t, qs=qs: lax.dot_general(k_chunk(t), qs, (((1,), (1,)), ((), ())),
                                                     preferred_element_type=F32)
        state = (jnp.full((1, r), -jnp.inf, F32), jnp.zeros((1, r), F32), jnp.zeros((dv, r), F32))
        scores = scores_of(0)
        if pending is not None:
            pending()
        for t in range(nk - 1):
            nxt = scores_of(t + 1)
            state = update(t, scores, state)
            scores = nxt

        def pending(n=n, scores=scores, state=state):
            _, l_run, acc = update(nk - 1, scores, state)
            finish(n, l_run, acc)
    pending()


def _diff_attn_kernel(q_ref, k_ref, vt_ref, lq1_ref, lk1_ref, lq2_ref, lk2_ref, g_ref, o_ref, *, bq, tiles, bk, nk,
                      lam_init):
    bt = bq // tiles
    lane = lax.broadcasted_iota(jnp.int32, (bt, HEAD_DIM), 1)
    zero = jnp.zeros((bt, HEAD_DIM), q_ref.dtype)
    q_tiles = []
    for n in range(tiles):
        q = q_ref[n * bt:(n + 1) * bt, :]
        q_tiles.append(jnp.concatenate([jnp.where(lane < HEAD_DIM // 2, q, zero),
                                        jnp.where(lane >= HEAD_DIM // 2, q, zero)], axis=0))
    lam = (jnp.exp(jnp.sum(lq1_ref[...] * lk1_ref[...], axis=1, keepdims=True))
           - jnp.exp(jnp.sum(lq2_ref[...] * lk2_ref[...], axis=1, keepdims=True)) + lam_init)

    def finish(n, l_run, acc):
        o_t = acc / l_run
        o = (o_t[:, :bt] - lam * o_t[:, bt:]).T
        y = o * lax.rsqrt(jnp.mean(o * o, axis=-1, keepdims=True) + EPS)
        o_ref[n * bt:(n + 1) * bt, :] = (y * g_ref[...] * (1 - lam_init)).astype(o_ref.dtype)

    _flash_tiles(q_tiles, lambda t: k_ref[t * bk:(t + 1) * bk, :], vt_ref, finish, nk=nk, bk=bk)


def _diff_attn_call(qk, vt, lam_vecs, subln, *, heads, q_rows, q_row0, kv_rows, kv_row0, lam_init, name):
    bq = _pick(q_rows, (512, 256, 128))
    tiles = max(1, bq // 256)
    bk = _pick(kv_rows, (2816, 768, 512, 640, 384, 256, 128))
    qb0, kb0 = q_row0 // bq, kv_row0 // kv_rows
    vec = pl.BlockSpec((1, HEAD_DIM // 2), lambda h, i: (0, 0))
    return pl.pallas_call(
        functools.partial(_diff_attn_kernel, bq=bq, tiles=tiles, bk=bk, nk=kv_rows // bk, lam_init=lam_init),
        grid=(heads, q_rows // bq),
        in_specs=[pl.BlockSpec((bq, HEAD_DIM), lambda h, i: (qb0 + i, h)),
                  pl.BlockSpec((kv_rows, HEAD_DIM), lambda h, i: (kb0, heads + h)),
                  pl.BlockSpec((HEAD_DIM, kv_rows), lambda h, i: (h, kb0)),
                  vec, vec, vec, vec,
                  pl.BlockSpec((1, HEAD_DIM), lambda h, i: (0, 0))],
        out_specs=pl.BlockSpec((bq, HEAD_DIM), lambda h, i: (i, h)),
        out_shape=jax.ShapeDtypeStruct((q_rows, heads * HEAD_DIM), BF16),
        compiler_params=_params("arbitrary", "arbitrary"),
        name=name,
    )(qk, qk, vt, *lam_vecs, subln)


def _mla_attn_kernel(qn_ref, qp_ref, kn_ref, kp_ref, vt_ref, o_ref, *, bq, tiles, bk, nk):
    bt = bq // tiles
    q_tiles = [jnp.concatenate([qn_ref[n * bt:(n + 1) * bt, :], qp_ref[n * bt:(n + 1) * bt, :]], axis=1)
               for n in range(tiles)]

    def k_chunk(t):
        rows = slice(t * bk, (t + 1) * bk)
        return jnp.concatenate([kn_ref[rows, :], kp_ref[rows, :]], axis=1)

    def finish(n, l_run, acc):
        o_ref[n * bt:(n + 1) * bt, :] = (acc / l_run).T.astype(o_ref.dtype)

    _flash_tiles(q_tiles, k_chunk, vt_ref, finish, nk=nk, bk=bk)


def _mla_attn_call(q, kn, kpe, vt, *, heads, q_rows, q_row0, kv_rows, kv_row0, name):
    bq = _pick(q_rows, (1024, 512, 256, 128))
    tiles = max(1, bq // 512)
    bk = _pick(kv_rows, (2816, 768, 512, 640, 384, 256, 128))
    qb0, kb0 = q_row0 // bq, kv_row0 // kv_rows
    return pl.pallas_call(
        functools.partial(_mla_attn_kernel, bq=bq, tiles=tiles, bk=bk, nk=kv_rows // bk),
        grid=(heads, q_rows // bq),
        in_specs=[pl.BlockSpec((bq, HEAD_DIM), lambda h, i: (qb0 + i, h)),
                  pl.BlockSpec((bq, HEAD_DIM), lambda h, i: (qb0 + i, heads + h)),
                  pl.BlockSpec((kv_rows, HEAD_DIM), lambda h, i: (kb0, h)),
                  pl.BlockSpec((kv_rows, LANES), lambda h, i: (kb0, 0)),
                  pl.BlockSpec((HEAD_DIM, kv_rows), lambda h, i: (h, kb0))],
        out_specs=pl.BlockSpec((bq, HEAD_DIM), lambda h, i: (i, h)),
        out_shape=jax.ShapeDtypeStruct((q_rows, heads * HEAD_DIM), BF16),
        compiler_params=_params("arbitrary", "arbitrary"),
        name=name,
    )(q, q, kn, kpe, vt)


def _rope_tables(seq, ctx_len, q_scale):
    rows = seq // GRID_W
    row = jnp.broadcast_to(jnp.arange(rows, dtype=F32)[:, None], (rows, GRID_W)).reshape(seq)
    col = jnp.broadcast_to(jnp.arange(GRID_W, dtype=F32)[None, :], (rows, GRID_W)).reshape(seq)
    n_freq = 16
    inv_freq = ROPE_BASE ** (-jnp.arange(n_freq, dtype=F32) / n_freq)
    ang = jnp.concatenate([row[:, None] * inv_freq, col[:, None] * inv_freq], axis=-1)
    cos, sin = jnp.cos(ang), jnp.sin(ang)
    zero = jnp.zeros_like(sin)
    c = jnp.concatenate([cos, cos, cos, cos], axis=-1)
    s1 = jnp.concatenate([-sin, zero, -sin, zero], axis=-1)
    s2 = jnp.concatenate([zero, sin, zero, sin], axis=-1)
    pad = lambda t, v: jnp.concatenate([t, jnp.full((ctx_len, LANES), v, F32)], axis=0)
    return tuple(jnp.concatenate([t, t * q_scale], axis=1) for t in (pad(c, 1.0), pad(s1, 0.0), pad(s2, 0.0)))


def kernel(x, c, ctx, c_ctx, ada_down, ada_up, ada_bias, norm_mix, norm_ffn, ffn_gate, ffn_up, ffn_down, ev_w_in, ev_conv, ev_lam_q1, ev_lam_k1, ev_lam_q2, ev_lam_k2, ev_subln, ev_w_out, od_w_in, od_q_norm, od_w_uq, od_kv_norm, od_w_ukv, od_pool_w, od_pool_scale, od_w_out, final_norm):
    _, seq, d = x.shape
    ctx_len = ctx.shape[1]
    m = seq + ctx_len
    depth = ada_down.shape[0]
    half = d // 2
    heads = half // HEAD_DIM
    q_rank = od_q_norm.shape[-1]
    kv_rank = od_kv_norm.shape[-1]
    rope_dim = HEAD_DIM // 2
    bf = lambda t: t.astype(BF16)

    r = jnp.concatenate([x[0], ctx[0]], axis=0)
    diff_q_scale = (HEAD_DIM // 2) ** -0.5 * LOG2E
    mla_q_scale = (HEAD_DIM + rope_dim) ** -0.5 * LOG2E
    tables_diff = _rope_tables(seq, ctx_len, diff_q_scale)
    tables_mla = _rope_tables(seq, ctx_len, mla_q_scale)

    cc = jnp.zeros((SUBLANES, d), F32).at[0].set(c[0]).at[1].set(c_ctx)
    mods = _ada_call(cc, bf(ada_down), bf(ada_up), ada_bias.reshape(depth, 1, -1))

    for l in range(depth):
        mod = mods[l]
        i = l // 2
        h = _norm_call(r, norm_mix[l], rows=m, out_dtype=BF16, mod=mod, shift_blk=0, scale_blk=1, n_lat=seq,
                       name="norm_mod_mix")
        if l % 2 == 0:
            lam_init = 0.8 - 0.6 * math.exp(-0.3 * l)
            pa = _wmm_call([h], [(ev_w_in, i, 0)], n=3 * half, out_dtype=F32, name="ev_in_conv")
            qk = _wmm_call([h], [(ev_w_in, i, 0)], n=2 * half, col0=3 * half, out_dtype=BF16, epi="rope",
                           rope=tables_diff, rope_cols=(0, 2 * half), rope_q_cols=(0, half), name="ev_in_qk")
            vt = _nt_call(bf(ev_w_in[i][:, 5 * half:].T), h, name="ev_in_vt")
            ya = _conv_call(pa, ev_conv[i], n_lat=seq)
            lam_vecs = [v[i].reshape(1, -1) for v in (ev_lam_q1, ev_lam_k1, ev_lam_q2, ev_lam_k2)]
            subln = ev_subln[i].reshape(1, -1)
            yb_lat = _diff_attn_call(qk, vt, lam_vecs, subln, heads=heads, q_rows=seq, q_row0=0, kv_rows=m,
                                     kv_row0=0, lam_init=lam_init, name="diff_attn")
            yb_ctx = _diff_attn_call(qk, vt, lam_vecs, subln, heads=heads, q_rows=ctx_len, q_row0=seq,
                                     kv_rows=ctx_len, kv_row0=seq, lam_init=lam_init, name="diff_attn_ctx")
            y1, y2 = ya, jnp.concatenate([yb_lat, yb_ctx], axis=0)
            w_out = ev_w_out
        else:
            w_in = od_w_in[i]
            w_kpe = jnp.pad(w_in[:, q_rank + kv_rank:q_rank + kv_rank + rope_dim], ((0, 0), (0, LANES - rope_dim)))
            cq = _mm_call(h, bf(w_in[:, :q_rank]), out_dtype=F32, name="od_in_cq")
            ckv = _mm_call(h, bf(w_in[:, q_rank:q_rank + kv_rank]), out_dtype=F32, name="od_in_ckv")
            kpe = _mm_call(h, bf(w_kpe), out_dtype=BF16, rope=tables_mla, rope_cols=(0, LANES), name="od_in_kpe")
            u = _mm_call(h, bf(w_in[:, q_rank + kv_rank + rope_dim:]), out_dtype=F32, name="od_in_pool")
            cqn = _norm_call(cq, od_q_norm[i], rows=m, out_dtype=BF16, name="norm_cq")
            ckvn = _norm_call(ckv, od_kv_norm[i], rows=m, out_dtype=BF16, name="norm_ckv")
            w_uq = od_w_uq[i].reshape(q_rank, heads, HEAD_DIM + rope_dim)
            w_qn = w_uq[:, :, :HEAD_DIM].reshape(q_rank, half)
            w_qp = jnp.pad(w_uq[:, :, HEAD_DIM:], ((0, 0), (0, 0), (0, LANES - rope_dim))).reshape(q_rank, heads * LANES)
            q = _mm_call(cqn, bf(jnp.concatenate([w_qn, w_qp], axis=1)), out_dtype=BF16, rope=tables_mla,
                         rope_cols=(half, half + heads * LANES), rope_q_cols=(half, half + heads * LANES),
                         plain_scale=mla_q_scale, name="od_uq")
            w_ukv = od_w_ukv[i].reshape(kv_rank, heads, 2 * HEAD_DIM)
            kn = _mm_call(ckvn, bf(w_ukv[:, :, :HEAD_DIM].reshape(kv_rank, half)), out_dtype=BF16, name="od_uk")
            vt = _nt_call(bf(w_ukv[:, :, HEAD_DIM:].reshape(kv_rank, half).T), ckvn, name="od_uvt")
            ym_lat = _mla_attn_call(q, kn, kpe, vt, heads=heads, q_rows=seq, q_row0=0, kv_rows=m, kv_row0=0,
                                    name="mla_attn")
            ym_ctx = _mla_attn_call(q, kn, kpe, vt, heads=heads, q_rows=ctx_len, q_row0=seq, kv_rows=ctx_len,
                                    kv_row0=seq, name="mla_attn_ctx")
            y1 = _pool_call(u, bf(od_pool_w[i]), od_pool_scale[i], n_lat=seq)
            y2 = jnp.concatenate([ym_lat, ym_ctx], axis=0)
            w_out = od_w_out
        r = _wmm_call([y1, y2], [(w_out, i, 0), (w_out, i, 1)], n=d, out_dtype=F32, epi="res", res=r, mod=mod,
                      gate_blk=2, n_lat=seq, name="mix_out")
        h2 = _norm_call(r, norm_ffn[l], rows=m, out_dtype=BF16, mod=mod, shift_blk=3, scale_blk=4, n_lat=seq,
                        name="norm_mod_ffn")
        a = _wmm_call([h2], [(ffn_gate, l, 0), (ffn_up, l, 0)], n=ffn_gate.shape[-1], out_dtype=BF16, epi="swiglu",
                      bm_cands=(1408, 1280, 768, 640, 512, 256, 128), bn_cands=(256, 128), name="ffn_gate_up")
        r = _mm_res_call(a, bf(ffn_down[l]), r, mod, 5, n_lat=seq, name="ffn_down")
    out = _norm_call(r, final_norm, rows=seq, out_dtype=F32, name="final_norm")
    return out[None]
```

```python
import functools
import math

import jax
import jax.numpy as jnp
from jax import lax
from jax.experimental import pallas as pl
from jax.experimental.pallas import tpu as pltpu

F32 = jnp.float32
BF16 = jnp.bfloat16

LANES = 128
SUBLANES = 8
HEAD_DIM = 128
GRID_W = 64
ROPE_BASE = 10000.0
EPS = 1e-6
POOL_WINDOWS = (2, 4, 8, 16)
POOL_HALO = 8
LOG2E = 1.4426950408889634
VMEM_LIMIT = 56 * 1024 * 1024


def _pick(n, cands):
    for c in cands:
        if c <= n and n % c == 0:
            return c
    return n


def _params(*sem):
    return pltpu.CompilerParams(dimension_semantics=sem, vmem_limit_bytes=VMEM_LIMIT)


def _is_ctx_rows(tile, bm, n_lat):
    rows = tile * bm + lax.broadcasted_iota(jnp.int32, (bm, 1), 0)
    return rows >= n_lat


def _sel(is_ctx, ref):
    return jnp.where(is_ctx, ref[1:2, :], ref[0:1, :])


def _ada_kernel(cc_ref, down_ref, up_ref, bias_ref, o_ref, t_ref):
    @pl.when(pl.program_id(1) == 0)
    def _():
        c = cc_ref[...]
        s = c * jax.nn.sigmoid(c)
        t = jnp.dot(s.astype(BF16), down_ref[...], preferred_element_type=F32)
        t_ref[...] = t.astype(BF16)

    o_ref[...] = jnp.dot(t_ref[...], up_ref[...], preferred_element_type=F32) + bias_ref[...]


def _ada_call(cc, down, up, bias):
    depth, d, rank = down.shape
    n = up.shape[-1]
    bn = _pick(n, (2048, 1024, 512, 256, 128))
    return pl.pallas_call(
        _ada_kernel,
        grid=(depth, n // bn),
        in_specs=[
            pl.BlockSpec((SUBLANES, d), lambda l, j: (0, 0)),
            pl.BlockSpec((None, d, rank), lambda l, j: (l, 0, 0)),
            pl.BlockSpec((None, rank, bn), lambda l, j: (l, 0, j)),
            pl.BlockSpec((None, 1, bn), lambda l, j: (l, 0, j)),
        ],
        out_specs=pl.BlockSpec((None, SUBLANES, bn), lambda l, j: (l, 0, j)),
        out_shape=jax.ShapeDtypeStruct((depth, SUBLANES, n), F32),
        scratch_shapes=[pltpu.VMEM((SUBLANES, rank), BF16)],
        compiler_params=_params("arbitrary", "arbitrary"),
        name="ada_mod",
    )(cc, down, up, bias)


NORM_ROWS = 16


def _norm_kernel(*refs, modulate, bm, n_lat):
    if modulate:
        x_ref, g_ref, sh_ref, sc_ref, o_ref = refs
    else:
        x_ref, g_ref, o_ref = refs
    g = g_ref[...]
    row0 = pl.program_id(0) * bm

    def step(r, _):
        rows = pl.ds(pl.multiple_of(r * NORM_ROWS, NORM_ROWS), NORM_ROWS)
        x = x_ref[rows, :].astype(F32)
        y = x * lax.rsqrt(jnp.mean(x * x, axis=-1, keepdims=True) + EPS)
        y = y * g
        if modulate:
            is_ctx = row0 + r * NORM_ROWS >= n_lat
            y = y * (1 + _sel(is_ctx, sc_ref)) + _sel(is_ctx, sh_ref)
        o_ref[rows, :] = y.astype(o_ref.dtype)
        return 0

    lax.fori_loop(0, bm // NORM_ROWS, step, 0, unroll=4)


def _norm_call(x, g, *, rows, out_dtype, mod=None, shift_blk=0, scale_blk=0, n_lat=0, name="rmsnorm"):
    k = x.shape[-1]
    bm = _pick(rows, (384, 320, 256, 128, 64, 32, 16, 8))
    in_specs = [pl.BlockSpec((bm, k), lambda i: (i, 0)), pl.BlockSpec((1, k), lambda i: (0, 0))]
    args = [x, g.reshape(1, k)]
    if mod is not None:
        in_specs += [pl.BlockSpec((SUBLANES, k), lambda i: (0, shift_blk)),
                     pl.BlockSpec((SUBLANES, k), lambda i: (0, scale_blk))]
        args += [mod, mod]
    return pl.pallas_call(
        functools.partial(_norm_kernel, modulate=mod is not None, bm=bm, n_lat=n_lat),
        grid=(rows // bm,),
        in_specs=in_specs,
        out_specs=pl.BlockSpec((bm, k), lambda i: (i, 0)),
        out_shape=jax.ShapeDtypeStruct((rows, k), out_dtype),
        compiler_params=_params("arbitrary"),
        name=name,
    )(*args)


def _rope(x, c_ref, s1_ref, s2_ref):
    c, s1, s2 = c_ref[...], s1_ref[...], s2_ref[...]
    out = []
    for b in range(x.shape[-1] // LANES):
        xb = x[:, b * LANES:(b + 1) * LANES]
        up = pltpu.roll(xb, LANES - 32, 1)
        dn = pltpu.roll(xb, 32, 1)
        out.append(xb * c + up * s1 + dn * s2)
    return jnp.concatenate(out, axis=1)


def _mm_kernel(*refs, rope_tiles, plain_scale):
    if rope_tiles is None:
        a_ref, w_ref, o_ref = refs
        o_ref[...] = jnp.dot(a_ref[...], w_ref[...], preferred_element_type=F32).astype(o_ref.dtype)
        return
    a_ref, w_ref, c_ref, s1_ref, s2_ref, o_ref = refs
    acc = jnp.dot(a_ref[...], w_ref[...], preferred_element_type=F32)
    j = pl.program_id(1)
    roped = (j >= rope_tiles[0]) & (j < rope_tiles[1])

    @pl.when(roped)
    def _():
        o_ref[...] = _rope(acc, c_ref, s1_ref, s2_ref).astype(o_ref.dtype)

    @pl.when(jnp.logical_not(roped))
    def _():
        o_ref[...] = (acc * plain_scale).astype(o_ref.dtype)


def _rope_table_specs(bm, bn, rope_q_cols, row_col):
    q0, q1 = (0, 0) if rope_q_cols is None else (rope_q_cols[0] // bn, rope_q_cols[1] // bn)

    def index(*grid):
        i, j = row_col(*grid)
        return i, jnp.where((j >= q0) & (j < q1), 1, 0)

    return [pl.BlockSpec((bm, LANES), index)] * 3


def _mm_call(a, w, *, out_dtype, rope=None, rope_cols=None, rope_q_cols=None, plain_scale=1.0, name="mm"):
    m, k = a.shape
    n = w.shape[-1]
    bm = _pick(m, (768, 640, 512, 384, 256, 128))
    bn = _pick(n, (512, 896, 256, 128))
    in_specs = [pl.BlockSpec((bm, k), lambda i, j: (i, 0)), pl.BlockSpec((k, bn), lambda i, j: (0, j))]
    args = [a, w]
    rope_tiles = None
    if rope is not None:
        rope_tiles = (rope_cols[0] // bn, rope_cols[1] // bn)
        in_specs += _rope_table_specs(bm, bn, rope_q_cols, lambda i, j: (i, j))
        args += list(rope)
    return pl.pallas_call(
        functools.partial(_mm_kernel, rope_tiles=rope_tiles, plain_scale=plain_scale),
        grid=(m // bm, n // bn),
        in_specs=in_specs,
        out_specs=pl.BlockSpec((bm, bn), lambda i, j: (i, j)),
        out_shape=jax.ShapeDtypeStruct((m, n), out_dtype),
        compiler_params=_params("arbitrary", "arbitrary"),
        name=name,
    )(*args)


def _mm_res_kernel(a_ref, w_ref, res_ref, gate_ref, o_ref, *, bm, n_lat):
    acc = jnp.dot(a_ref[...], w_ref[...], preferred_element_type=F32)
    gate = _sel(_is_ctx_rows(pl.program_id(0), bm, n_lat), gate_ref)
    o_ref[...] = res_ref[...] + gate * acc


def _mm_res_call(a, w, layer, res, mod, gate_blk, *, n_lat, name):
    m, n = res.shape
    k = a.shape[-1]
    bm = _pick(m, (768, 640, 384, 256, 128))
    bn = _pick(n, (256, 128))
    return pl.pallas_call(
        functools.partial(_mm_res_kernel, bm=bm, n_lat=n_lat),
        grid=(m // bm, n // bn),
        in_specs=[pl.BlockSpec((bm, k), lambda i, j: (i, 0)),
                  pl.BlockSpec((None, k, bn), lambda i, j: (layer, 0, j)),
                  pl.BlockSpec((bm, bn), lambda i, j: (i, j)),
                  pl.BlockSpec((SUBLANES, bn), lambda i, j: (0, gate_blk * (n // bn) + j))],
        out_specs=pl.BlockSpec((bm, bn), lambda i, j: (i, j)),
        out_shape=jax.ShapeDtypeStruct((m, n), F32),
        compiler_params=_params("arbitrary", "arbitrary"),
        name=name,
    )(a, w, res, mod)


CAST_ROWS = 256


def _cast_weights(w_refs, wb_refs):
    for w_ref, wb_ref in zip(w_refs, wb_refs):
        nrows = math.gcd(w_ref.shape[0], CAST_ROWS)

        def step(r, _, w_ref=w_ref, wb_ref=wb_ref, nrows=nrows):
            rows = pl.ds(pl.multiple_of(r * nrows, nrows), nrows)
            wb_ref[rows, :] = w_ref[rows, :].astype(BF16)
            return 0
        lax.fori_loop(0, w_ref.shape[0] // nrows, step, 0)


def _wmm_kernel(*refs, parts, epi, bm, n_lat, rope_tiles):
    n_extra = {"plain": 0, "transpose": 0, "rope": 3, "res": 2, "swiglu": 0}[epi]
    n_w = 2 * parts if epi == "swiglu" else parts
    a_refs, w_refs = refs[:parts], refs[parts:parts + n_w]
    extra = refs[parts + n_w:parts + n_w + n_extra]
    o_ref = refs[parts + n_w + n_extra]
    wb_refs = refs[parts + n_w + n_extra + 1:]
    j, i = pl.program_id(0), pl.program_id(1)

    @pl.when(i == 0)
    def _():
        _cast_weights(w_refs, wb_refs)

    def mm(wbs):
        acc = jnp.dot(a_refs[0][...], wbs[0][...], preferred_element_type=F32)
        for a_ref, wb_ref in zip(a_refs[1:], wbs[1:]):
            acc = acc + jnp.dot(a_ref[...], wb_ref[...], preferred_element_type=F32)
        return acc

    if epi == "swiglu":
        g = mm(wb_refs[:parts])
        u = mm(wb_refs[parts:])
        o_ref[...] = (g * jax.nn.sigmoid(g) * u).astype(o_ref.dtype)
    elif epi == "res":
        res_ref, gate_ref = extra
        gate = _sel(_is_ctx_rows(i, bm, n_lat), gate_ref)
        o_ref[...] = res_ref[...] + gate * mm(wb_refs)
    elif epi == "rope":
        acc = mm(wb_refs)
        roped = (j >= rope_tiles[0]) & (j < rope_tiles[1])

        @pl.when(roped)
        def _():
            o_ref[...] = _rope(acc, *extra).astype(o_ref.dtype)

        @pl.when(jnp.logical_not(roped))
        def _():
            o_ref[...] = acc.astype(o_ref.dtype)
    elif epi == "transpose":
        o_ref[...] = mm(wb_refs).T.astype(o_ref.dtype)
    else:
        o_ref[...] = mm(wb_refs).astype(o_ref.dtype)


def _wmm_call(a_list, weights, *, n, col0=0, out_dtype, epi="plain", rope=None, rope_cols=None, rope_q_cols=None,
              res=None, mod=None,
              gate_blk=0, n_lat=0, bm_cands=(768, 640, 512, 384, 256, 128), bn_cands=(512, 256, 128), name="wmm"):
    m = a_list[0].shape[0]
    parts = len(a_list)
    bm = _pick(m, bm_cands)
    bn = _pick(math.gcd(n, col0) if col0 else n, bn_cands)
    cb0 = col0 // bn
    in_specs = [pl.BlockSpec((bm, a.shape[1]), lambda j, i: (i, 0)) for a in a_list]
    args = list(a_list)
    scratch = []
    for idx, (w, layer, rb) in enumerate(weights):
        k = a_list[idx % parts].shape[1]
        if layer is None:
            in_specs.append(pl.BlockSpec((k, bn), lambda j, i, rb=rb: (rb, cb0 + j)))
        else:
            in_specs.append(pl.BlockSpec((None, k, bn), lambda j, i, layer=layer, rb=rb: (layer, rb, cb0 + j)))
        args.append(w)
        scratch.append(pltpu.VMEM((k, bn), BF16))
    rope_tiles = None
    if epi == "rope":
        rope_tiles = (rope_cols[0] // bn, rope_cols[1] // bn)
        in_specs += _rope_table_specs(bm, bn, rope_q_cols, lambda j, i: (i, j))
        args += list(rope)
    elif epi == "res":
        in_specs += [pl.BlockSpec((bm, bn), lambda j, i: (i, j)),
                     pl.BlockSpec((SUBLANES, bn), lambda j, i: (0, gate_blk * (n // bn) + j))]
        args += [res, mod]
    return pl.pallas_call(
        functools.partial(_wmm_kernel, parts=parts, epi=epi, bm=bm, n_lat=n_lat, rope_tiles=rope_tiles),
        grid=(n // bn, m // bm),
        in_specs=in_specs,
        out_specs=(pl.BlockSpec((bn, bm), lambda j, i: (j, i)) if epi == "transpose"
                   else pl.BlockSpec((bm, bn), lambda j, i: (i, j))),
        out_shape=jax.ShapeDtypeStruct((n, m) if epi == "transpose" else (m, n), out_dtype),
        scratch_shapes=scratch,
        compiler_params=_params("arbitrary", "arbitrary"),
        name=name,
    )(*args)


def _conv_kernel(bg_ref, cg_ref, xa_ref, cgp_ref, xap_ref, cgn_ref, xan_ref, w_ref, o_ref, *, bm, n_lat, m):
    i = pl.program_id(0)
    u = cg_ref[...] * xa_ref[...]
    u_prev = cgp_ref[SUBLANES - 1:SUBLANES, :] * xap_ref[SUBLANES - 1:SUBLANES, :]
    u_next = cgn_ref[0:1, :] * xan_ref[0:1, :]
    loc = lax.broadcasted_iota(jnp.int32, (bm, 1), 0)
    rows = i * bm + loc
    prev = jnp.where(loc == 0, u_prev, pltpu.roll(u, 1, 0))
    prev = jnp.where((rows == 0) | (rows == n_lat), 0.0, prev)
    nxt = jnp.where(loc == bm - 1, u_next, pltpu.roll(u, bm - 1, 0))
    nxt = jnp.where((rows == n_lat - 1) | (rows == m - 1), 0.0, nxt)
    y = w_ref[0:1, :] * prev + w_ref[1:2, :] * u + w_ref[2:3, :] * nxt
    o_ref[...] = (bg_ref[...] * y).astype(o_ref.dtype)


def _conv_call(pa, conv_w, *, n_lat):
    m = pa.shape[0]
    aw = conv_w.shape[-1]
    bm = _pick(math.gcd(n_lat, m - n_lat), (256, 128, 64, 32, 16, 8))
    bc = _pick(aw, (512, 256, 128))
    nc = aw // bc
    rb = bm // SUBLANES
    last = m // SUBLANES - 1
    main = lambda off: pl.BlockSpec((bm, bc), lambda i, c: (i, off * nc + c))
    prev = lambda off: pl.BlockSpec((SUBLANES, bc), lambda i, c: (jnp.maximum(i * rb - 1, 0), off * nc + c))
    nxt = lambda off: pl.BlockSpec((SUBLANES, bc), lambda i, c: (jnp.minimum((i + 1) * rb, last), off * nc + c))
    return pl.pallas_call(
        functools.partial(_conv_kernel, bm=bm, n_lat=n_lat, m=m),
        grid=(m // bm, nc),
        in_specs=[main(0), main(1), main(2), prev(1), prev(2), nxt(1), nxt(2),
                  pl.BlockSpec((conv_w.shape[0], bc), lambda i, c: (0, c))],
        out_specs=pl.BlockSpec((bm, bc), lambda i, c: (i, c)),
        out_shape=jax.ShapeDtypeStruct((m, aw), BF16),
        compiler_params=_params("arbitrary", "arbitrary"),
        name="short_conv",
    )(pa, pa, pa, pa, pa, pa, pa, conv_w)


def _pool_kernel(u_ref, up_ref, un_ref, w_ref, sc_ref, o_ref, *, bm, n_lat, m, pg):
    i = pl.program_id(0)
    ext = bm + 2 * POOL_HALO
    is_ctx = i * bm >= n_lat
    lo = jnp.where(is_ctx, n_lat, 0)
    hi = jnp.where(is_ctx, m, n_lat)
    rows_ext = i * bm - POOL_HALO + lax.broadcasted_iota(jnp.int32, (ext, 1), 0)
    valid = (rows_ext >= lo) & (rows_ext < hi)
    t = i * bm + lax.broadcasted_iota(jnp.int32, (bm, 1), 0) - lo
    t_len = hi - lo
    for g, win in enumerate(POOL_WINDOWS):
        cols = slice(g * pg, (g + 1) * pg)
        u = u_ref[:, cols]
        e = jnp.concatenate([up_ref[:, cols], u, un_ref[:, cols]], axis=0)
        e = jnp.where(valid, e, 0.0)
        span = 1
        while span < win:
            e = e + pltpu.roll(e, ext - span, 0)
            span *= 2
        start = POOL_HALO - win // 2
        if start:
            e = pltpu.roll(e, ext - start, 0)
        wsum = e[0:bm, :]
        w_lo = jnp.clip(t - win // 2, 0, t_len - 1)
        w_hi = jnp.clip(t + win // 2 - 1, 0, t_len - 1)
        cnt = (w_hi - w_lo + 1).astype(F32)
        diff = (wsum / cnt - u).astype(BF16)
        y = jnp.dot(diff, w_ref[g], preferred_element_type=F32) * sc_ref[:, cols]
        o_ref[:, cols] = y.astype(o_ref.dtype)


def _pool_call(u, pool_w, pool_scale, *, n_lat):
    m, pw = u.shape
    pg = pw // len(POOL_WINDOWS)
    bm = _pick(math.gcd(n_lat, m - n_lat), (256, 128, 64, 32, 16, 8))
    rb = bm // SUBLANES
    last = m // SUBLANES - 1
    return pl.pallas_call(
        functools.partial(_pool_kernel, bm=bm, n_lat=n_lat, m=m, pg=pg),
        grid=(m // bm,),
        in_specs=[pl.BlockSpec((bm, pw), lambda i: (i, 0)),
                  pl.BlockSpec((SUBLANES, pw), lambda i: (jnp.maximum(i * rb - 1, 0), 0)),
                  pl.BlockSpec((SUBLANES, pw), lambda i: (jnp.minimum((i + 1) * rb, last), 0)),
                  pl.BlockSpec(pool_w.shape, lambda i: (0, 0, 0)),
                  pl.BlockSpec((1, pw), lambda i: (0, 0))],
        out_specs=pl.BlockSpec((bm, pw), lambda i: (i, 0)),
        out_shape=jax.ShapeDtypeStruct((m, pw), BF16),
        compiler_params=_params("arbitrary"),
        name="multiscale_pool",
    )(u, u, u, pool_w, pool_scale.reshape(1, pw))


def _flash_tiles(q_tiles, k_chunk, vt_ref, finish, *, nk, bk):
    dv = vt_ref.shape[0]

    def update(t, s, state):
        m_run, l_run, acc = state
        m_new = jnp.maximum(m_run, jnp.max(s, axis=0, keepdims=True))
        alpha = jnp.exp2(m_run - m_new)
        p = jnp.exp2(s - m_new)
        l_new = alpha * l_run + jnp.sum(p, axis=0, keepdims=True)
        acc = alpha * acc + jnp.dot(vt_ref[:, t * bk:(t + 1) * bk], p.astype(BF16), preferred_element_type=F32)
        return m_new, l_new, acc

    pending = None
    for n, qs in enumerate(q_tiles):
        r = qs.shape[0]
        scores_of = lambda t, qs=qs: lax.dot_general(k_chunk(t), qs, (((1,), (1,)), ((), ())),
                                                     preferred_element_type=F32)
        state = (jnp.full((1, r), -jnp.inf, F32), jnp.zeros((1, r), F32), jnp.zeros((dv, r), F32))
        scores = scores_of(0)
        if pending is not None:
            pending()
        for t in range(nk - 1):
            nxt = scores_of(t + 1)
            state = update(t, scores, state)
            scores = nxt

        def pending(n=n, scores=scores, state=state):
            _, l_run, acc = update(nk - 1, scores, state)
            finish(n, l_run, acc)
    pending()


def _diff_attn_kernel(q_ref, k_ref, vt_ref, lq1_ref, lk1_ref, lq2_ref, lk2_ref, g_ref, *rest, bq, tiles, bk, nk,
                      lam_init):
    o_ref = rest[-1]
    bt = bq // tiles
    lane = lax.broadcasted_iota(jnp.int32, (bt, HEAD_DIM), 1)
    zero = jnp.zeros((bt, HEAD_DIM), q_ref.dtype)
    q_tiles = []
    for n in range(tiles):
        q = q_ref[n * bt:(n + 1) * bt, :]
        q_tiles.append(jnp.concatenate([jnp.where(lane < HEAD_DIM // 2, q, zero),
                                        jnp.where(lane >= HEAD_DIM // 2, q, zero)], axis=0))
    lam = (jnp.exp(jnp.sum(lq1_ref[...] * lk1_ref[...], axis=1, keepdims=True))
           - jnp.exp(jnp.sum(lq2_ref[...] * lk2_ref[...], axis=1, keepdims=True)) + lam_init)

    def finish(n, l_run, acc):
        o_t = acc / l_run
        o = (o_t[:, :bt] - lam * o_t[:, bt:]).T
        y = o * lax.rsqrt(jnp.mean(o * o, axis=-1, keepdims=True) + EPS)
        o_ref[n * bt:(n + 1) * bt, :] = (y * g_ref[...] * (1 - lam_init)).astype(o_ref.dtype)

    _flash_tiles(q_tiles, lambda t: k_ref[t * bk:(t + 1) * bk, :], vt_ref, finish, nk=nk, bk=bk)


def _out_alias(out, out_rows, heads, n_in):
    shape = jax.ShapeDtypeStruct((out_rows, heads * HEAD_DIM), BF16)
    if out is None:
        return shape, [], [], {}
    return shape, [pl.BlockSpec(memory_space=pl.ANY)], [out], {n_in: 0}


def _diff_attn_call(qk, vt, lam_vecs, subln, *, heads, q_rows, q_row0, kv_rows, kv_row0, lam_init, out_rows, out=None,
                    name):
    bq = _pick(q_rows, (512, 256, 128))
    tiles = max(1, bq // 256)
    bk = _pick(kv_rows, (2816, 768, 512, 640, 384, 256, 128))
    qb0, kb0 = q_row0 // bq, kv_row0 // kv_rows
    vec = pl.BlockSpec((1, HEAD_DIM // 2), lambda h, i: (0, 0))
    out_shape, alias_specs, alias_args, aliases = _out_alias(out, out_rows, heads, 8)
    return pl.pallas_call(
        functools.partial(_diff_attn_kernel, bq=bq, tiles=tiles, bk=bk, nk=kv_rows // bk, lam_init=lam_init),
        grid=(heads, q_rows // bq),
        in_specs=[pl.BlockSpec((bq, HEAD_DIM), lambda h, i: (qb0 + i, h)),
                  pl.BlockSpec((kv_rows, HEAD_DIM), lambda h, i: (kb0, heads + h)),
                  pl.BlockSpec((HEAD_DIM, kv_rows), lambda h, i: (h, kb0)),
                  vec, vec, vec, vec,
                  pl.BlockSpec((1, HEAD_DIM), lambda h, i: (0, 0))] + alias_specs,
        out_specs=pl.BlockSpec((bq, HEAD_DIM), lambda h, i: (qb0 + i, h)),
        out_shape=out_shape,
        input_output_aliases=aliases,
        compiler_params=_params("arbitrary", "arbitrary"),
        name=name,
    )(qk, qk, vt, *lam_vecs, subln, *alias_args)


def _mla_attn_kernel(qn_ref, qp_ref, kn_ref, kp_ref, vt_ref, *rest, bq, tiles, bk, nk):
    o_ref = rest[-1]
    bt = bq // tiles
    q_tiles = [jnp.concatenate([qn_ref[n * bt:(n + 1) * bt, :], qp_ref[n * bt:(n + 1) * bt, :]], axis=1)
               for n in range(tiles)]

    def k_chunk(t):
        rows = slice(t * bk, (t + 1) * bk)
        return jnp.concatenate([kn_ref[rows, :], kp_ref[rows, :]], axis=1)

    def finish(n, l_run, acc):
        o_ref[n * bt:(n + 1) * bt, :] = (acc / l_run).T.astype(o_ref.dtype)

    _flash_tiles(q_tiles, k_chunk, vt_ref, finish, nk=nk, bk=bk)


def _mla_attn_call(q, kn, kpe, vt, *, heads, q_rows, q_row0, kv_rows, kv_row0, out_rows, out=None, name):
    bq = _pick(q_rows, (1024, 512, 256, 128))
    tiles = max(1, bq // 512)
    bk = _pick(kv_rows, (2816, 768, 512, 640, 384, 256, 128))
    qb0, kb0 = q_row0 // bq, kv_row0 // kv_rows
    out_shape, alias_specs, alias_args, aliases = _out_alias(out, out_rows, heads, 5)
    return pl.pallas_call(
        functools.partial(_mla_attn_kernel, bq=bq, tiles=tiles, bk=bk, nk=kv_rows // bk),
        grid=(heads, q_rows // bq),
        in_specs=[pl.BlockSpec((bq, HEAD_DIM), lambda h, i: (qb0 + i, h)),
                  pl.BlockSpec((bq, HEAD_DIM), lambda h, i: (qb0 + i, heads + h)),
                  pl.BlockSpec((kv_rows, HEAD_DIM), lambda h, i: (kb0, h)),
                  pl.BlockSpec((kv_rows, LANES), lambda h, i: (kb0, 0)),
                  pl.BlockSpec((HEAD_DIM, kv_rows), lambda h, i: (h, kb0))] + alias_specs,
        out_specs=pl.BlockSpec((bq, HEAD_DIM), lambda h, i: (qb0 + i, h)),
        out_shape=out_shape,
        input_output_aliases=aliases,
        compiler_params=_params("arbitrary", "arbitrary"),
        name=name,
    )(q, q, kn, kpe, vt, *alias_args)


def _rope_tables(seq, ctx_len, q_scale):
    rows = seq // GRID_W
    row = jnp.broadcast_to(jnp.arange(rows, dtype=F32)[:, None], (rows, GRID_W)).reshape(seq)
    col = jnp.broadcast_to(jnp.arange(GRID_W, dtype=F32)[None, :], (rows, GRID_W)).reshape(seq)
    n_freq = 16
    inv_freq = ROPE_BASE ** (-jnp.arange(n_freq, dtype=F32) / n_freq)
    ang = jnp.concatenate([row[:, None] * inv_freq, col[:, None] * inv_freq], axis=-1)
    cos, sin = jnp.cos(ang), jnp.sin(ang)
    zero = jnp.zeros_like(sin)
    c = jnp.concatenate([cos, cos, cos, cos], axis=-1)
    s1 = jnp.concatenate([-sin, zero, -sin, zero], axis=-1)
    s2 = jnp.concatenate([zero, sin, zero, sin], axis=-1)
    pad = lambda t, v: jnp.concatenate([t, jnp.full((ctx_len, LANES), v, F32)], axis=0)
    return tuple(jnp.concatenate([t, t * q_scale], axis=1) for t in (pad(c, 1.0), pad(s1, 0.0), pad(s2, 0.0)))


def kernel(x, c, ctx, c_ctx, ada_down, ada_up, ada_bias, norm_mix, norm_ffn, ffn_gate, ffn_up, ffn_down, ev_w_in, ev_conv, ev_lam_q1, ev_lam_k1, ev_lam_q2, ev_lam_k2, ev_subln, ev_w_out, od_w_in, od_q_norm, od_w_uq, od_kv_norm, od_w_ukv, od_pool_w, od_pool_scale, od_w_out, final_norm):
    _, seq, d = x.shape
    ctx_len = ctx.shape[1]
    m = seq + ctx_len
    depth = ada_down.shape[0]
    half = d // 2
    heads = half // HEAD_DIM
    q_rank = od_q_norm.shape[-1]
    kv_rank = od_kv_norm.shape[-1]
    rope_dim = HEAD_DIM // 2
    bf = lambda t: t.astype(BF16)

    r = jnp.concatenate([x[0], ctx[0]], axis=0)
    diff_q_scale = (HEAD_DIM // 2) ** -0.5 * LOG2E
    mla_q_scale = (HEAD_DIM + rope_dim) ** -0.5 * LOG2E
    tables_diff = _rope_tables(seq, ctx_len, diff_q_scale)
    tables_mla = _rope_tables(seq, ctx_len, mla_q_scale)

    cc = jnp.zeros((SUBLANES, d), F32).at[0].set(c[0]).at[1].set(c_ctx)
    mods = _ada_call(cc, bf(ada_down), bf(ada_up), ada_bias.reshape(depth, 1, -1))

    ffn_down_bf = bf(ffn_down)
    for l in range(depth):
        mod = mods[l]
        i = l // 2
        h = _norm_call(r, norm_mix[l], rows=m, out_dtype=BF16, mod=mod, shift_blk=0, scale_blk=1, n_lat=seq,
                       name="norm_mod_mix")
        if l % 2 == 0:
            lam_init = 0.8 - 0.6 * math.exp(-0.3 * l)
            pa = _wmm_call([h], [(ev_w_in, i, 0)], n=3 * half, out_dtype=F32, name="ev_in_conv")
            qk = _wmm_call([h], [(ev_w_in, i, 0)], n=2 * half, col0=3 * half, out_dtype=BF16, epi="rope",
                           rope=tables_diff, rope_cols=(0, 2 * half), rope_q_cols=(0, half), name="ev_in_qk")
            vt = _wmm_call([h], [(ev_w_in, i, 0)], n=half, col0=5 * half, out_dtype=BF16, epi="transpose",
                           name="ev_in_vt")
            ya = _conv_call(pa, ev_conv[i], n_lat=seq)
            lam_vecs = [v[i].reshape(1, -1) for v in (ev_lam_q1, ev_lam_k1, ev_lam_q2, ev_lam_k2)]
            subln = ev_subln[i].reshape(1, -1)
            yb = _diff_attn_call(qk, vt, lam_vecs, subln, heads=heads, q_rows=seq, q_row0=0, kv_rows=m, kv_row0=0,
                                 lam_init=lam_init, out_rows=m, name="diff_attn")
            yb = _diff_attn_call(qk, vt, lam_vecs, subln, heads=heads, q_rows=ctx_len, q_row0=seq, kv_rows=ctx_len,
                                 kv_row0=seq, lam_init=lam_init, out_rows=m, out=yb, name="diff_attn_ctx")
            y1, y2 = ya, yb
            w_out = ev_w_out
        else:
            w_in = od_w_in[i]
            w_kpe = jnp.pad(w_in[:, q_rank + kv_rank:q_rank + kv_rank + rope_dim], ((0, 0), (0, LANES - rope_dim)))
            cq = _mm_call(h, bf(w_in[:, :q_rank]), out_dtype=F32, name="od_in_cq")
            ckv = _mm_call(h, bf(w_in[:, q_rank:q_rank + kv_rank]), out_dtype=F32, name="od_in_ckv")
            kpe = _mm_call(h, bf(w_kpe), out_dtype=BF16, rope=tables_mla, rope_cols=(0, LANES), name="od_in_kpe")
            u = _mm_call(h, bf(w_in[:, q_rank + kv_rank + rope_dim:]), out_dtype=F32, name="od_in_pool")
            cqn = _norm_call(cq, od_q_norm[i], rows=m, out_dtype=BF16, name="norm_cq")
            ckvn = _norm_call(ckv, od_kv_norm[i], rows=m, out_dtype=BF16, name="norm_ckv")
            w_uq = od_w_uq[i].reshape(q_rank, heads, HEAD_DIM + rope_dim)
            w_qn = w_uq[:, :, :HEAD_DIM].reshape(q_rank, half)
            w_qp = jnp.pad(w_uq[:, :, HEAD_DIM:], ((0, 0), (0, 0), (0, LANES - rope_dim))).reshape(q_rank, heads * LANES)
            q = _mm_call(cqn, bf(jnp.concatenate([w_qn, w_qp], axis=1)), out_dtype=BF16, rope=tables_mla,
                         rope_cols=(half, half + heads * LANES), rope_q_cols=(half, half + heads * LANES),
                         plain_scale=mla_q_scale, name="od_uq")
            w_ukv = od_w_ukv[i].reshape(kv_rank, heads, 2 * HEAD_DIM)
            kn = _mm_call(ckvn, bf(w_ukv[:, :, :HEAD_DIM].reshape(kv_rank, half)), out_dtype=BF16, name="od_uk")
            vt = _wmm_call([ckvn], [(w_ukv[:, :, HEAD_DIM:].reshape(kv_rank, half), None, 0)], n=half, out_dtype=BF16,
                           epi="transpose", name="od_uvt")
            ym = _mla_attn_call(q, kn, kpe, vt, heads=heads, q_rows=seq, q_row0=0, kv_rows=m, kv_row0=0,
                                out_rows=m, name="mla_attn")
            ym = _mla_attn_call(q, kn, kpe, vt, heads=heads, q_rows=ctx_len, q_row0=seq, kv_rows=ctx_len,
                                kv_row0=seq, out_rows=m, out=ym, name="mla_attn_ctx")
            y1 = _pool_call(u, bf(od_pool_w[i]), od_pool_scale[i], n_lat=seq)
            y2 = ym
            w_out = od_w_out
        r = _wmm_call([y1, y2], [(w_out, i, 0), (w_out, i, 1)], n=d, out_dtype=F32, epi="res", res=r, mod=mod,
                      gate_blk=2, n_lat=seq, name="mix_out")
        h2 = _norm_call(r, norm_ffn[l], rows=m, out_dtype=BF16, mod=mod, shift_blk=3, scale_blk=4, n_lat=seq,
                        name="norm_mod_ffn")
        a = _wmm_call([h2], [(ffn_gate, l, 0), (ffn_up, l, 0)], n=ffn_gate.shape[-1], out_dtype=BF16, epi="swiglu",
                      bm_cands=(1408, 1280, 768, 640, 512, 256, 128), bn_cands=(256, 128), name="ffn_gate_up")
        r = _mm_res_call(a, ffn_down_bf, l, r, mod, 5, n_lat=seq, name="ffn_down")
    out = _norm_call(r, final_norm, rows=seq, out_dtype=F32, name="final_norm")
    return out[None]
```

```python
import functools
import math

import jax
import jax.numpy as jnp
from jax import lax
from jax.experimental import pallas as pl
from jax.experimental.pallas import tpu as pltpu

F32 = jnp.float32
BF16 = jnp.bfloat16

LANES = 128
SUBLANES = 8
HEAD_DIM = 128
GRID_W = 64
ROPE_BASE = 10000.0
EPS = 1e-6
POOL_WINDOWS = (2, 4, 8, 16)
POOL_HALO = 8
LOG2E = 1.4426950408889634
VMEM_LIMIT = 56 * 1024 * 1024


def _pick(n, cands):
    for c in cands:
        if c <= n and n % c == 0:
            return c
    return n


def _params(*sem):
    return pltpu.CompilerParams(dimension_semantics=sem, vmem_limit_bytes=VMEM_LIMIT)


def _is_ctx_rows(tile, bm, n_lat):
    rows = tile * bm + lax.broadcasted_iota(jnp.int32, (bm, 1), 0)
    return rows >= n_lat


def _sel(is_ctx, ref):
    return jnp.where(is_ctx, ref[1:2, :], ref[0:1, :])


def _ada_kernel(cc_ref, down_ref, up_ref, bias_ref, o_ref, t_ref):
    @pl.when(pl.program_id(1) == 0)
    def _():
        c = cc_ref[...]
        s = c * jax.nn.sigmoid(c)
        t = jnp.dot(s.astype(BF16), down_ref[...].astype(BF16), preferred_element_type=F32)
        t_ref[...] = t.astype(BF16)

    o_ref[...] = jnp.dot(t_ref[...], up_ref[...].astype(BF16), preferred_element_type=F32) + bias_ref[...]


def _ada_call(cc, down, up, bias):
    depth, d, rank = down.shape
    n = up.shape[-1]
    bn = _pick(n, (2048, 1024, 512, 256, 128))
    return pl.pallas_call(
        _ada_kernel,
        grid=(depth, n // bn),
        in_specs=[
            pl.BlockSpec((SUBLANES, d), lambda l, j: (0, 0)),
            pl.BlockSpec((None, d, rank), lambda l, j: (l, 0, 0)),
            pl.BlockSpec((None, rank, bn), lambda l, j: (l, 0, j)),
            pl.BlockSpec((None, 1, bn), lambda l, j: (l, 0, j)),
        ],
        out_specs=pl.BlockSpec((None, SUBLANES, bn), lambda l, j: (l, 0, j)),
        out_shape=jax.ShapeDtypeStruct((depth, SUBLANES, n), F32),
        scratch_shapes=[pltpu.VMEM((SUBLANES, rank), BF16)],
        compiler_params=_params("arbitrary", "arbitrary"),
        name="ada_mod",
    )(cc, down, up, bias)


NORM_ROWS = 16


def _norm_kernel(*refs, modulate, bm, n_lat):
    if modulate:
        x_ref, g_ref, sh_ref, sc_ref, o_ref = refs
    else:
        x_ref, g_ref, o_ref = refs
    g = g_ref[...]
    row0 = pl.program_id(0) * bm

    def step(r, _):
        rows = pl.ds(pl.multiple_of(r * NORM_ROWS, NORM_ROWS), NORM_ROWS)
        x = x_ref[rows, :].astype(F32)
        y = x * lax.rsqrt(jnp.mean(x * x, axis=-1, keepdims=True) + EPS)
        y = y * g
        if modulate:
            is_ctx = row0 + r * NORM_ROWS >= n_lat
            y = y * (1 + _sel(is_ctx, sc_ref)) + _sel(is_ctx, sh_ref)
        o_ref[rows, :] = y.astype(o_ref.dtype)
        return 0

    lax.fori_loop(0, bm // NORM_ROWS, step, 0, unroll=4)


def _norm_call(x, g, *, rows, out_dtype, mod=None, shift_blk=0, scale_blk=0, n_lat=0, name="rmsnorm"):
    k = x.shape[-1]
    bm = _pick(rows, (384, 320, 256, 128, 64, 32, 16, 8))
    in_specs = [pl.BlockSpec((bm, k), lambda i: (i, 0)), pl.BlockSpec((1, k), lambda i: (0, 0))]
    args = [x, g.reshape(1, k)]
    if mod is not None:
        in_specs += [pl.BlockSpec((SUBLANES, k), lambda i: (0, shift_blk)),
                     pl.BlockSpec((SUBLANES, k), lambda i: (0, scale_blk))]
        args += [mod, mod]
    return pl.pallas_call(
        functools.partial(_norm_kernel, modulate=mod is not None, bm=bm, n_lat=n_lat),
        grid=(rows // bm,),
        in_specs=in_specs,
        out_specs=pl.BlockSpec((bm, k), lambda i: (i, 0)),
        out_shape=jax.ShapeDtypeStruct((rows, k), out_dtype),
        compiler_params=_params("arbitrary"),
        name=name,
    )(*args)


def _rope(x, c_ref, s1_ref, s2_ref):
    c, s1, s2 = c_ref[...], s1_ref[...], s2_ref[...]
    out = []
    for b in range(x.shape[-1] // LANES):
        xb = x[:, b * LANES:(b + 1) * LANES]
        up = pltpu.roll(xb, LANES - 32, 1)
        dn = pltpu.roll(xb, 32, 1)
        out.append(xb * c + up * s1 + dn * s2)
    return jnp.concatenate(out, axis=1)


def _mm_kernel(*refs, rope_tiles, plain_scale):
    if rope_tiles is None:
        a_ref, w_ref, o_ref = refs
        o_ref[...] = jnp.dot(a_ref[...], w_ref[...], preferred_element_type=F32).astype(o_ref.dtype)
        return
    a_ref, w_ref, c_ref, s1_ref, s2_ref, o_ref = refs
    acc = jnp.dot(a_ref[...], w_ref[...], preferred_element_type=F32)
    j = pl.program_id(1)
    roped = (j >= rope_tiles[0]) & (j < rope_tiles[1])

    @pl.when(roped)
    def _():
        o_ref[...] = _rope(acc, c_ref, s1_ref, s2_ref).astype(o_ref.dtype)

    @pl.when(jnp.logical_not(roped))
    def _():
        o_ref[...] = (acc * plain_scale).astype(o_ref.dtype)


def _rope_table_specs(bm, bn, rope_q_cols, row_col):
    q0, q1 = (0, 0) if rope_q_cols is None else (rope_q_cols[0] // bn, rope_q_cols[1] // bn)

    def index(*grid):
        i, j = row_col(*grid)
        return i, jnp.where((j >= q0) & (j < q1), 1, 0)

    return [pl.BlockSpec((bm, LANES), index)] * 3


def _mm_call(a, w, *, out_dtype, rope=None, rope_cols=None, rope_q_cols=None, plain_scale=1.0, name="mm"):
    m, k = a.shape
    n = w.shape[-1]
    bm = _pick(m, (768, 640, 512, 384, 256, 128))
    bn = _pick(n, (512, 896, 256, 128))
    in_specs = [pl.BlockSpec((bm, k), lambda i, j: (i, 0)), pl.BlockSpec((k, bn), lambda i, j: (0, j))]
    args = [a, w]
    rope_tiles = None
    if rope is not None:
        rope_tiles = (rope_cols[0] // bn, rope_cols[1] // bn)
        in_specs += _rope_table_specs(bm, bn, rope_q_cols, lambda i, j: (i, j))
        args += list(rope)
    return pl.pallas_call(
        functools.partial(_mm_kernel, rope_tiles=rope_tiles, plain_scale=plain_scale),
        grid=(m // bm, n // bn),
        in_specs=in_specs,
        out_specs=pl.BlockSpec((bm, bn), lambda i, j: (i, j)),
        out_shape=jax.ShapeDtypeStruct((m, n), out_dtype),
        compiler_params=_params("arbitrary", "arbitrary"),
        name=name,
    )(*args)


def _mm_res_kernel(a_ref, w_ref, res_ref, gate_ref, o_ref, *, bm, n_lat):
    acc = jnp.dot(a_ref[...], w_ref[...], preferred_element_type=F32)
    gate = _sel(_is_ctx_rows(pl.program_id(0), bm, n_lat), gate_ref)
    o_ref[...] = res_ref[...] + gate * acc


def _mm_res_call(a, w, res, mod, gate_blk, *, n_lat, name):
    m, n = res.shape
    k = a.shape[-1]
    bm = _pick(m, (768, 640, 384, 256, 128))
    bn = _pick(n, (256, 128))
    return pl.pallas_call(
        functools.partial(_mm_res_kernel, bm=bm, n_lat=n_lat),
        grid=(m // bm, n // bn),
        in_specs=[pl.BlockSpec((bm, k), lambda i, j: (i, 0)),
                  pl.BlockSpec((k, bn), lambda i, j: (0, j)),
                  pl.BlockSpec((bm, bn), lambda i, j: (i, j)),
                  pl.BlockSpec((SUBLANES, bn), lambda i, j: (0, gate_blk * (n // bn) + j))],
        out_specs=pl.BlockSpec((bm, bn), lambda i, j: (i, j)),
        out_shape=jax.ShapeDtypeStruct((m, n), F32),
        compiler_params=_params("arbitrary", "arbitrary"),
        name=name,
    )(a, w, res, mod)


CAST_ROWS = 256


def _cast_weights(w_refs, wb_refs):
    for w_ref, wb_ref in zip(w_refs, wb_refs):
        nrows = math.gcd(w_ref.shape[0], CAST_ROWS)

        def step(r, _, w_ref=w_ref, wb_ref=wb_ref, nrows=nrows):
            rows = pl.ds(pl.multiple_of(r * nrows, nrows), nrows)
            wb_ref[rows, :] = w_ref[rows, :].astype(BF16)
            return 0
        lax.fori_loop(0, w_ref.shape[0] // nrows, step, 0)


def _wmm_kernel(*refs, parts, epi, bm, n_lat, rope_tiles, side):
    n_extra = {"plain": 0, "transpose": 0, "rope": 3, "res": 2, "swiglu": 0}[epi]
    n_w = 2 * parts if epi == "swiglu" else parts
    a_refs, w_refs = refs[:parts], refs[parts:parts + n_w]
    extra = refs[parts + n_w:parts + n_w + n_extra]
    rest = refs[parts + n_w + n_extra:]
    if side:
        side_in, o_ref, side_out = rest[:3]
        wb_refs = rest[3:]
        side_out[...] = side_in[...].astype(BF16)
    else:
        o_ref, wb_refs = rest[0], rest[1:]
    j, i = pl.program_id(0), pl.program_id(1)

    @pl.when(i == 0)
    def _():
        _cast_weights(w_refs, wb_refs)

    def mm(wbs):
        acc = jnp.dot(a_refs[0][...], wbs[0][...], preferred_element_type=F32)
        for a_ref, wb_ref in zip(a_refs[1:], wbs[1:]):
            acc = acc + jnp.dot(a_ref[...], wb_ref[...], preferred_element_type=F32)
        return acc

    if epi == "swiglu":
        g = mm(wb_refs[:parts])
        u = mm(wb_refs[parts:])
        o_ref[...] = (g * jax.nn.sigmoid(g) * u).astype(o_ref.dtype)
    elif epi == "res":
        res_ref, gate_ref = extra
        gate = _sel(_is_ctx_rows(i, bm, n_lat), gate_ref)
        o_ref[...] = res_ref[...] + gate * mm(wb_refs)
    elif epi == "rope":
        acc = mm(wb_refs)
        roped = (j >= rope_tiles[0]) & (j < rope_tiles[1])

        @pl.when(roped)
        def _():
            o_ref[...] = _rope(acc, *extra).astype(o_ref.dtype)

        @pl.when(jnp.logical_not(roped))
        def _():
            o_ref[...] = acc.astype(o_ref.dtype)
    elif epi == "transpose":
        o_ref[...] = mm(wb_refs).T.astype(o_ref.dtype)
    else:
        o_ref[...] = mm(wb_refs).astype(o_ref.dtype)


def _wmm_call(a_list, weights, *, n, col0=0, out_dtype, epi="plain", rope=None, rope_cols=None, rope_q_cols=None,
              res=None, mod=None,
              gate_blk=0, n_lat=0, bm_cands=(768, 640, 512, 384, 256, 128), bn_cands=(512, 256, 128),
              side_cast=None, name="wmm"):
    m = a_list[0].shape[0]
    parts = len(a_list)
    bm = _pick(m, bm_cands)
    bn = _pick(math.gcd(n, col0) if col0 else n, bn_cands)
    cb0 = col0 // bn
    in_specs = [pl.BlockSpec((bm, a.shape[1]), lambda j, i: (i, 0)) for a in a_list]
    args = list(a_list)
    scratch = []
    for idx, (w, layer, rb) in enumerate(weights):
        k = a_list[idx % parts].shape[1]
        if layer is None:
            in_specs.append(pl.BlockSpec((k, bn), lambda j, i, rb=rb: (rb, cb0 + j)))
        else:
            in_specs.append(pl.BlockSpec((None, k, bn), lambda j, i, layer=layer, rb=rb: (layer, rb, cb0 + j)))
        args.append(w)
        scratch.append(pltpu.VMEM((k, bn), BF16))
    rope_tiles = None
    if epi == "rope":
        rope_tiles = (rope_cols[0] // bn, rope_cols[1] // bn)
        in_specs += _rope_table_specs(bm, bn, rope_q_cols, lambda j, i: (i, j))
        args += list(rope)
    elif epi == "res":
        in_specs += [pl.BlockSpec((bm, bn), lambda j, i: (i, j)),
                     pl.BlockSpec((SUBLANES, bn), lambda j, i: (0, gate_blk * (n // bn) + j))]
        args += [res, mod]
    out_specs = (pl.BlockSpec((bn, bm), lambda j, i: (j, i)) if epi == "transpose"
                 else pl.BlockSpec((bm, bn), lambda j, i: (i, j)))
    out_shape = jax.ShapeDtypeStruct((n, m) if epi == "transpose" else (m, n), out_dtype)
    if side_cast is not None:
        w_side, side_layer = side_cast
        _, side_rows, side_cols = w_side.shape
        steps, ni = (n // bn) * (m // bm), m // bm
        slab = next(r for r in (64, 128, 256, 512, 1024, side_rows) if side_rows % r == 0 and side_rows // r <= steps)
        last = side_rows // slab - 1
        in_specs.append(pl.BlockSpec((None, slab, side_cols),
                                     lambda j, i: (side_layer, jnp.minimum(j * ni + i, last), 0)))
        args.append(w_side)
        out_specs = (out_specs, pl.BlockSpec((slab, side_cols), lambda j, i: (jnp.minimum(j * ni + i, last), 0)))
        out_shape = (out_shape, jax.ShapeDtypeStruct((side_rows, side_cols), BF16))
    return pl.pallas_call(
        functools.partial(_wmm_kernel, parts=parts, epi=epi, bm=bm, n_lat=n_lat, rope_tiles=rope_tiles,
                          side=side_cast is not None),
        grid=(n // bn, m // bm),
        in_specs=in_specs,
        out_specs=out_specs,
        out_shape=out_shape,
        scratch_shapes=scratch,
        compiler_params=_params("arbitrary", "arbitrary"),
        name=name,
    )(*args)


def _conv_kernel(bg_ref, cg_ref, xa_ref, cgp_ref, xap_ref, cgn_ref, xan_ref, w_ref, o_ref, *, bm, n_lat, m):
    i = pl.program_id(0)
    u = cg_ref[...] * xa_ref[...]
    u_prev = cgp_ref[SUBLANES - 1:SUBLANES, :] * xap_ref[SUBLANES - 1:SUBLANES, :]
    u_next = cgn_ref[0:1, :] * xan_ref[0:1, :]
    loc = lax.broadcasted_iota(jnp.int32, (bm, 1), 0)
    rows = i * bm + loc
    prev = jnp.where(loc == 0, u_prev, pltpu.roll(u, 1, 0))
    prev = jnp.where((rows == 0) | (rows == n_lat), 0.0, prev)
    nxt = jnp.where(loc == bm - 1, u_next, pltpu.roll(u, bm - 1, 0))
    nxt = jnp.where((rows == n_lat - 1) | (rows == m - 1), 0.0, nxt)
    y = w_ref[0:1, :] * prev + w_ref[1:2, :] * u + w_ref[2:3, :] * nxt
    o_ref[...] = (bg_ref[...] * y).astype(o_ref.dtype)


def _conv_call(pa, conv_w, *, n_lat):
    m = pa.shape[0]
    aw = conv_w.shape[-1]
    bm = _pick(math.gcd(n_lat, m - n_lat), (256, 128, 64, 32, 16, 8))
    bc = _pick(aw, (512, 256, 128))
    nc = aw // bc
    rb = bm // SUBLANES
    last = m // SUBLANES - 1
    main = lambda off: pl.BlockSpec((bm, bc), lambda i, c: (i, off * nc + c))
    prev = lambda off: pl.BlockSpec((SUBLANES, bc), lambda i, c: (jnp.maximum(i * rb - 1, 0), off * nc + c))
    nxt = lambda off: pl.BlockSpec((SUBLANES, bc), lambda i, c: (jnp.minimum((i + 1) * rb, last), off * nc + c))
    return pl.pallas_call(
        functools.partial(_conv_kernel, bm=bm, n_lat=n_lat, m=m),
        grid=(m // bm, nc),
        in_specs=[main(0), main(1), main(2), prev(1), prev(2), nxt(1), nxt(2),
                  pl.BlockSpec((conv_w.shape[0], bc), lambda i, c: (0, c))],
        out_specs=pl.BlockSpec((bm, bc), lambda i, c: (i, c)),
        out_shape=jax.ShapeDtypeStruct((m, aw), BF16),
        compiler_params=_params("arbitrary", "arbitrary"),
        name="short_conv",
    )(pa, pa, pa, pa, pa, pa, pa, conv_w)


def _pool_kernel(u_ref, up_ref, un_ref, w_ref, sc_ref, o_ref, *, bm, n_lat, m, pg):
    i = pl.program_id(0)
    ext = bm + 2 * POOL_HALO
    is_ctx = i * bm >= n_lat
    lo = jnp.where(is_ctx, n_lat, 0)
    hi = jnp.where(is_ctx, m, n_lat)
    rows_ext = i * bm - POOL_HALO + lax.broadcasted_iota(jnp.int32, (ext, 1), 0)
    valid = (rows_ext >= lo) & (rows_ext < hi)
    t = i * bm + lax.broadcasted_iota(jnp.int32, (bm, 1), 0) - lo
    t_len = hi - lo
    for g, win in enumerate(POOL_WINDOWS):
        cols = slice(g * pg, (g + 1) * pg)
        u = u_ref[:, cols]
        e = jnp.concatenate([up_ref[:, cols], u, un_ref[:, cols]], axis=0)
        e = jnp.where(valid, e, 0.0)
        span = 1
        while span < win:
            e = e + pltpu.roll(e, ext - span, 0)
            span *= 2
        start = POOL_HALO - win // 2
        if start:
            e = pltpu.roll(e, ext - start, 0)
        wsum = e[0:bm, :]
        w_lo = jnp.clip(t - win // 2, 0, t_len - 1)
        w_hi = jnp.clip(t + win // 2 - 1, 0, t_len - 1)
        cnt = (w_hi - w_lo + 1).astype(F32)
        diff = (wsum / cnt - u).astype(BF16)
        y = jnp.dot(diff, w_ref[g], preferred_element_type=F32) * sc_ref[:, cols]
        o_ref[:, cols] = y.astype(o_ref.dtype)


def _pool_call(u, pool_w, pool_scale, *, n_lat):
    m, pw = u.shape
    pg = pw // len(POOL_WINDOWS)
    bm = _pick(math.gcd(n_lat, m - n_lat), (256, 128, 64, 32, 16, 8))
    rb = bm // SUBLANES
    last = m // SUBLANES - 1
    return pl.pallas_call(
        functools.partial(_pool_kernel, bm=bm, n_lat=n_lat, m=m, pg=pg),
        grid=(m // bm,),
        in_specs=[pl.BlockSpec((bm, pw), lambda i: (i, 0)),
                  pl.BlockSpec((SUBLANES, pw), lambda i: (jnp.maximum(i * rb - 1, 0), 0)),
                  pl.BlockSpec((SUBLANES, pw), lambda i: (jnp.minimum((i + 1) * rb, last), 0)),
                  pl.BlockSpec(pool_w.shape, lambda i: (0, 0, 0)),
                  pl.BlockSpec((1, pw), lambda i: (0, 0))],
        out_specs=pl.BlockSpec((bm, pw), lambda i: (i, 0)),
        out_shape=jax.ShapeDtypeStruct((m, pw), BF16),
        compiler_params=_params("arbitrary"),
        name="multiscale_pool",
    )(u, u, u, pool_w, pool_scale.reshape(1, pw))


def _flash_tiles(q_tiles, k_chunk, vt_ref, finish, *, nk, bk):
    dv = vt_ref.shape[0]

    def update(t, s, state):
        m_run, l_run, acc = state
        m_new = jnp.maximum(m_run, jnp.max(s, axis=0, keepdims=True))
        alpha = jnp.exp2(m_run - m_new)
        p = jnp.exp2(s - m_new)
        l_new = alpha * l_run + jnp.sum(p, axis=0, keepdims=True)
        acc = alpha * acc + jnp.dot(vt_ref[:, t * bk:(t + 1) * bk], p.astype(BF16), preferred_element_type=F32)
        return m_new, l_new, acc

    pending = None
    for n, qs in enumerate(q_tiles):
        r = qs.shape[0]
        scores_of = lambda t, qs=qs: lax.dot_general(k_chunk(t), qs, (((1,), (1,)), ((), ())),
                                                     preferred_element_type=F32)
        state = (jnp.full((1, r), -jnp.inf, F32), jnp.zeros((1, r), F32), jnp.zeros((dv, r), F32))
        scores = scores_of(0)
        if pending is not None:
            pending()
        for t in range(nk - 1):
            nxt = scores_of(t + 1)
            state = update(t, scores, state)
            scores = nxt

        def pending(n=n, scores=scores, state=state):
            _, l_run, acc = update(nk - 1, scores, state)
            finish(n, l_run, acc)
    pending()


def _diff_attn_kernel(q_ref, k_ref, vt_ref, lq1_ref, lk1_ref, lq2_ref, lk2_ref, g_ref, *rest, bq, tiles, bk, nk,
                      lam_init):
    o_ref = rest[-1]
    bt = bq // tiles
    lane = lax.broadcasted_iota(jnp.int32, (bt, HEAD_DIM), 1)
    zero = jnp.zeros((bt, HEAD_DIM), q_ref.dtype)
    q_tiles = []
    for n in range(tiles):
        q = q_ref[n * bt:(n + 1) * bt, :]
        q_tiles.append(jnp.concatenate([jnp.where(lane < HEAD_DIM // 2, q, zero),
                                        jnp.where(lane >= HEAD_DIM // 2, q, zero)], axis=0))
    lam = (jnp.exp(jnp.sum(lq1_ref[...] * lk1_ref[...], axis=1, keepdims=True))
           - jnp.exp(jnp.sum(lq2_ref[...] * lk2_ref[...], axis=1, keepdims=True)) + lam_init)

    def finish(n, l_run, acc):
        o_t = acc / l_run
        o = (o_t[:, :bt] - lam * o_t[:, bt:]).T
        y = o * lax.rsqrt(jnp.mean(o * o, axis=-1, keepdims=True) + EPS)
        o_ref[n * bt:(n + 1) * bt, :] = (y * g_ref[...] * (1 - lam_init)).astype(o_ref.dtype)

    _flash_tiles(q_tiles, lambda t: k_ref[t * bk:(t + 1) * bk, :], vt_ref, finish, nk=nk, bk=bk)


def _out_alias(out, out_rows, heads, n_in):
    shape = jax.ShapeDtypeStruct((out_rows, heads * HEAD_DIM), BF16)
    if out is None:
        return shape, [], [], {}
    return shape, [pl.BlockSpec(memory_space=pl.ANY)], [out], {n_in: 0}


def _diff_attn_call(qk, vt, lam_vecs, subln, *, heads, q_rows, q_row0, kv_rows, kv_row0, lam_init, out_rows, out=None,
                    name):
    bq = _pick(q_rows, (512, 256, 128))
    tiles = max(1, bq // 256)
    bk = _pick(kv_rows, (2816, 768, 512, 640, 384, 256, 128))
    qb0, kb0 = q_row0 // bq, kv_row0 // kv_rows
    vec = pl.BlockSpec((1, HEAD_DIM // 2), lambda h, i: (0, 0))
    out_shape, alias_specs, alias_args, aliases = _out_alias(out, out_rows, heads, 8)
    return pl.pallas_call(
        functools.partial(_diff_attn_kernel, bq=bq, tiles=tiles, bk=bk, nk=kv_rows // bk, lam_init=lam_init),
        grid=(heads, q_rows // bq),
        in_specs=[pl.BlockSpec((bq, HEAD_DIM), lambda h, i: (qb0 + i, h)),
                  pl.BlockSpec((kv_rows, HEAD_DIM), lambda h, i: (kb0, heads + h)),
                  pl.BlockSpec((HEAD_DIM, kv_rows), lambda h, i: (h, kb0)),
                  vec, vec, vec, vec,
                  pl.BlockSpec((1, HEAD_DIM), lambda h, i: (0, 0))] + alias_specs,
        out_specs=pl.BlockSpec((bq, HEAD_DIM), lambda h, i: (qb0 + i, h)),
        out_shape=out_shape,
        input_output_aliases=aliases,
        compiler_params=_params("arbitrary", "arbitrary"),
        name=name,
    )(qk, qk, vt, *lam_vecs, subln, *alias_args)


def _mla_attn_kernel(qn_ref, qp_ref, kn_ref, kp_ref, vt_ref, *rest, bq, tiles, bk, nk):
    o_ref = rest[-1]
    bt = bq // tiles
    q_tiles = [jnp.concatenate([qn_ref[n * bt:(n + 1) * bt, :], qp_ref[n * bt:(n + 1) * bt, :]], axis=1)
               for n in range(tiles)]

    def k_chunk(t):
        rows = slice(t * bk, (t + 1) * bk)
        return jnp.concatenate([kn_ref[rows, :], kp_ref[rows, :]], axis=1)

    def finish(n, l_run, acc):
        o_ref[n * bt:(n + 1) * bt, :] = (acc / l_run).T.astype(o_ref.dtype)

    _flash_tiles(q_tiles, k_chunk, vt_ref, finish, nk=nk, bk=bk)


def _mla_attn_call(q, kn, kpe, vt, *, heads, q_rows, q_row0, kv_rows, kv_row0, out_rows, out=None, name):
    bq = _pick(q_rows, (1024, 512, 256, 128))
    tiles = max(1, bq // 512)
    bk = _pick(kv_rows, (2816, 768, 512, 640, 384, 256, 128))
    qb0, kb0 = q_row0 // bq, kv_row0 // kv_rows
    out_shape, alias_specs, alias_args, aliases = _out_alias(out, out_rows, heads, 5)
    return pl.pallas_call(
        functools.partial(_mla_attn_kernel, bq=bq, tiles=tiles, bk=bk, nk=kv_rows // bk),
        grid=(heads, q_rows // bq),
        in_specs=[pl.BlockSpec((bq, HEAD_DIM), lambda h, i: (qb0 + i, h)),
                  pl.BlockSpec((bq, HEAD_DIM), lambda h, i: (qb0 + i, heads + h)),
                  pl.BlockSpec((kv_rows, HEAD_DIM), lambda h, i: (kb0, h)),
                  pl.BlockSpec((kv_rows, LANES), lambda h, i: (kb0, 0)),
                  pl.BlockSpec((HEAD_DIM, kv_rows), lambda h, i: (h, kb0))] + alias_specs,
        out_specs=pl.BlockSpec((bq, HEAD_DIM), lambda h, i: (qb0 + i, h)),
        out_shape=out_shape,
        input_output_aliases=aliases,
        compiler_params=_params("arbitrary", "arbitrary"),
        name=name,
    )(q, q, kn, kpe, vt, *alias_args)


def _rope_tables(seq, ctx_len, q_scale):
    rows = seq // GRID_W
    row = jnp.broadcast_to(jnp.arange(rows, dtype=F32)[:, None], (rows, GRID_W)).reshape(seq)
    col = jnp.broadcast_to(jnp.arange(GRID_W, dtype=F32)[None, :], (rows, GRID_W)).reshape(seq)
    n_freq = 16
    inv_freq = ROPE_BASE ** (-jnp.arange(n_freq, dtype=F32) / n_freq)
    ang = jnp.concatenate([row[:, None] * inv_freq, col[:, None] * inv_freq], axis=-1)
    cos, sin = jnp.cos(ang), jnp.sin(ang)
    zero = jnp.zeros_like(sin)
    c = jnp.concatenate([cos, cos, cos, cos], axis=-1)
    s1 = jnp.concatenate([-sin, zero, -sin, zero], axis=-1)
    s2 = jnp.concatenate([zero, sin, zero, sin], axis=-1)
    pad = lambda t, v: jnp.concatenate([t, jnp.full((ctx_len, LANES), v, F32)], axis=0)
    return tuple(jnp.concatenate([t, t * q_scale], axis=1) for t in (pad(c, 1.0), pad(s1, 0.0), pad(s2, 0.0)))


def kernel(x, c, ctx, c_ctx, ada_down, ada_up, ada_bias, norm_mix, norm_ffn, ffn_gate, ffn_up, ffn_down, ev_w_in, ev_conv, ev_lam_q1, ev_lam_k1, ev_lam_q2, ev_lam_k2, ev_subln, ev_w_out, od_w_in, od_q_norm, od_w_uq, od_kv_norm, od_w_ukv, od_pool_w, od_pool_scale, od_w_out, final_norm):
    _, seq, d = x.shape
    ctx_len = ctx.shape[1]
    m = seq + ctx_len
    depth = ada_down.shape[0]
    half = d // 2
    heads = half // HEAD_DIM
    q_rank = od_q_norm.shape[-1]
    kv_rank = od_kv_norm.shape[-1]
    rope_dim = HEAD_DIM // 2
    bf = lambda t: t.astype(BF16)

    r = jnp.concatenate([x[0], ctx[0]], axis=0)
    diff_q_scale = (HEAD_DIM // 2) ** -0.5 * LOG2E
    mla_q_scale = (HEAD_DIM + rope_dim) ** -0.5 * LOG2E
    tables_diff = _rope_tables(seq, ctx_len, diff_q_scale)
    tables_mla = _rope_tables(seq, ctx_len, mla_q_scale)

    cc = jnp.zeros((SUBLANES, d), F32).at[0].set(c[0]).at[1].set(c_ctx)
    mods = _ada_call(cc, ada_down, ada_up, ada_bias.reshape(depth, 1, -1))

    for l in range(depth):
        mod = mods[l]
        i = l // 2
        h = _norm_call(r, norm_mix[l], rows=m, out_dtype=BF16, mod=mod, shift_blk=0, scale_blk=1, n_lat=seq,
                       name="norm_mod_mix")
        if l % 2 == 0:
            lam_init = 0.8 - 0.6 * math.exp(-0.3 * l)
            pa = _wmm_call([h], [(ev_w_in, i, 0)], n=3 * half, out_dtype=F32, name="ev_in_conv")
            qk = _wmm_call([h], [(ev_w_in, i, 0)], n=2 * half, col0=3 * half, out_dtype=BF16, epi="rope",
                           rope=tables_diff, rope_cols=(0, 2 * half), rope_q_cols=(0, half), name="ev_in_qk")
            vt = _wmm_call([h], [(ev_w_in, i, 0)], n=half, col0=5 * half, out_dtype=BF16, epi="transpose",
                           name="ev_in_vt")
            ya = _conv_call(pa, ev_conv[i], n_lat=seq)
            lam_vecs = [v[i].reshape(1, -1) for v in (ev_lam_q1, ev_lam_k1, ev_lam_q2, ev_lam_k2)]
            subln = ev_subln[i].reshape(1, -1)
            yb = _diff_attn_call(qk, vt, lam_vecs, subln, heads=heads, q_rows=seq, q_row0=0, kv_rows=m, kv_row0=0,
                                 lam_init=lam_init, out_rows=m, name="diff_attn")
            yb = _diff_attn_call(qk, vt, lam_vecs, subln, heads=heads, q_rows=ctx_len, q_row0=seq, kv_rows=ctx_len,
                                 kv_row0=seq, lam_init=lam_init, out_rows=m, out=yb, name="diff_attn_ctx")
            y1, y2 = ya, yb
            w_out = ev_w_out
        else:
            w_in = od_w_in[i]
            w_kpe = jnp.pad(w_in[:, q_rank + kv_rank:q_rank + kv_rank + rope_dim], ((0, 0), (0, LANES - rope_dim)))
            cq = _mm_call(h, bf(w_in[:, :q_rank]), out_dtype=F32, name="od_in_cq")
            ckv = _mm_call(h, bf(w_in[:, q_rank:q_rank + kv_rank]), out_dtype=F32, name="od_in_ckv")
            kpe = _mm_call(h, bf(w_kpe), out_dtype=BF16, rope=tables_mla, rope_cols=(0, LANES), name="od_in_kpe")
            u = _mm_call(h, bf(w_in[:, q_rank + kv_rank + rope_dim:]), out_dtype=F32, name="od_in_pool")
            cqn = _norm_call(cq, od_q_norm[i], rows=m, out_dtype=BF16, name="norm_cq")
            ckvn = _norm_call(ckv, od_kv_norm[i], rows=m, out_dtype=BF16, name="norm_ckv")
            w_uq = od_w_uq[i].reshape(q_rank, heads, HEAD_DIM + rope_dim)
            w_qn = w_uq[:, :, :HEAD_DIM].reshape(q_rank, half)
            w_qp = jnp.pad(w_uq[:, :, HEAD_DIM:], ((0, 0), (0, 0), (0, LANES - rope_dim))).reshape(q_rank, heads * LANES)
            q = _mm_call(cqn, bf(jnp.concatenate([w_qn, w_qp], axis=1)), out_dtype=BF16, rope=tables_mla,
                         rope_cols=(half, half + heads * LANES), rope_q_cols=(half, half + heads * LANES),
                         plain_scale=mla_q_scale, name="od_uq")
            w_ukv = od_w_ukv[i].reshape(kv_rank, heads, 2 * HEAD_DIM)
            kn = _mm_call(ckvn, bf(w_ukv[:, :, :HEAD_DIM].reshape(kv_rank, half)), out_dtype=BF16, name="od_uk")
            vt = _wmm_call([ckvn], [(w_ukv[:, :, HEAD_DIM:].reshape(kv_rank, half), None, 0)], n=half, out_dtype=BF16,
                           epi="transpose", name="od_uvt")
            ym = _mla_attn_call(q, kn, kpe, vt, heads=heads, q_rows=seq, q_row0=0, kv_rows=m, kv_row0=0,
                                out_rows=m, name="mla_attn")
            ym = _mla_attn_call(q, kn, kpe, vt, heads=heads, q_rows=ctx_len, q_row0=seq, kv_rows=ctx_len,
                                kv_row0=seq, out_rows=m, out=ym, name="mla_attn_ctx")
            y1 = _pool_call(u, bf(od_pool_w[i]), od_pool_scale[i], n_lat=seq)
            y2 = ym
            w_out = od_w_out
        r = _wmm_call([y1, y2], [(w_out, i, 0), (w_out, i, 1)], n=d, out_dtype=F32, epi="res", res=r, mod=mod,
                      gate_blk=2, n_lat=seq, name="mix_out")
        h2 = _norm_call(r, norm_ffn[l], rows=m, out_dtype=BF16, mod=mod, shift_blk=3, scale_blk=4, n_lat=seq,
                        name="norm_mod_ffn")
        a, w_down = _wmm_call([h2], [(ffn_gate, l, 0), (ffn_up, l, 0)], n=ffn_gate.shape[-1], out_dtype=BF16,
                              epi="swiglu", bm_cands=(1408, 1280, 768, 640, 512, 256, 128), bn_cands=(256, 128),
                              side_cast=(ffn_down, l), name="ffn_gate_up")
        r = _mm_res_call(a, w_down, r, mod, 5, n_lat=seq, name="ffn_down")
    out = _norm_call(r, final_norm, rows=seq, out_dtype=F32, name="final_norm")
    return out[None]
```

```python
import functools
import math

import jax
import jax.numpy as jnp
from jax import lax
from jax.experimental import pallas as pl
from jax.experimental.pallas import tpu as pltpu

F32 = jnp.float32
BF16 = jnp.bfloat16

LANES = 128
SUBLANES = 8
HEAD_DIM = 128
GRID_W = 64
ROPE_BASE = 10000.0
EPS = 1e-6
POOL_WINDOWS = (2, 4, 8, 16)
POOL_HALO = 8
LOG2E = 1.4426950408889634
VMEM_LIMIT = 56 * 1024 * 1024


def _pick(n, cands):
    for c in cands:
        if c <= n and n % c == 0:
            return c
    return n


def _params(*sem):
    return pltpu.CompilerParams(dimension_semantics=sem, vmem_limit_bytes=VMEM_LIMIT)


def _is_ctx_rows(tile, bm, n_lat):
    rows = tile * bm + lax.broadcasted_iota(jnp.int32, (bm, 1), 0)
    return rows >= n_lat


def _sel(is_ctx, ref):
    return jnp.where(is_ctx, ref[1:2, :], ref[0:1, :])


def _ada_kernel(cc_ref, down_ref, up_ref, bias_ref, o_ref, t_ref):
    @pl.when(pl.program_id(1) == 0)
    def _():
        c = cc_ref[...]
        s = c * jax.nn.sigmoid(c)
        t = jnp.dot(s.astype(BF16), down_ref[...].astype(BF16), preferred_element_type=F32)
        t_ref[...] = t.astype(BF16)

    o_ref[...] = jnp.dot(t_ref[...], up_ref[...].astype(BF16), preferred_element_type=F32) + bias_ref[...]


def _ada_call(cc, down, up, bias):
    depth, d, rank = down.shape
    n = up.shape[-1]
    bn = _pick(n, (2048, 1024, 512, 256, 128))
    return pl.pallas_call(
        _ada_kernel,
        grid=(depth, n // bn),
        in_specs=[
            pl.BlockSpec((SUBLANES, d), lambda l, j: (0, 0)),
            pl.BlockSpec((None, d, rank), lambda l, j: (l, 0, 0)),
            pl.BlockSpec((None, rank, bn), lambda l, j: (l, 0, j)),
            pl.BlockSpec((None, 1, bn), lambda l, j: (l, 0, j)),
        ],
        out_specs=pl.BlockSpec((None, SUBLANES, bn), lambda l, j: (l, 0, j)),
        out_shape=jax.ShapeDtypeStruct((depth, SUBLANES, n), F32),
        scratch_shapes=[pltpu.VMEM((SUBLANES, rank), BF16)],
        compiler_params=_params("arbitrary", "arbitrary"),
        name="ada_mod",
    )(cc, down, up, bias)


NORM_ROWS = 16


def _norm_kernel(*refs, modulate, bm, n_lat):
    if modulate:
        x_ref, g_ref, sh_ref, sc_ref, o_ref = refs
    else:
        x_ref, g_ref, o_ref = refs
    g = g_ref[...]
    row0 = pl.program_id(0) * bm

    def step(r, _):
        rows = pl.ds(pl.multiple_of(r * NORM_ROWS, NORM_ROWS), NORM_ROWS)
        x = x_ref[rows, :].astype(F32)
        y = x * lax.rsqrt(jnp.mean(x * x, axis=-1, keepdims=True) + EPS)
        y = y * g
        if modulate:
            is_ctx = row0 + r * NORM_ROWS >= n_lat
            y = y * (1 + _sel(is_ctx, sc_ref)) + _sel(is_ctx, sh_ref)
        o_ref[rows, :] = y.astype(o_ref.dtype)
        return 0

    lax.fori_loop(0, bm // NORM_ROWS, step, 0, unroll=4)


def _norm_call(x, g, *, rows, out_dtype, mod=None, shift_blk=0, scale_blk=0, n_lat=0, name="rmsnorm"):
    k = x.shape[-1]
    bm = _pick(rows, (384, 320, 256, 128, 64, 32, 16, 8))
    in_specs = [pl.BlockSpec((bm, k), lambda i: (i, 0)), pl.BlockSpec((1, k), lambda i: (0, 0))]
    args = [x, g.reshape(1, k)]
    if mod is not None:
        in_specs += [pl.BlockSpec((SUBLANES, k), lambda i: (0, shift_blk)),
                     pl.BlockSpec((SUBLANES, k), lambda i: (0, scale_blk))]
        args += [mod, mod]
    return pl.pallas_call(
        functools.partial(_norm_kernel, modulate=mod is not None, bm=bm, n_lat=n_lat),
        grid=(rows // bm,),
        in_specs=in_specs,
        out_specs=pl.BlockSpec((bm, k), lambda i: (i, 0)),
        out_shape=jax.ShapeDtypeStruct((rows, k), out_dtype),
        compiler_params=_params("arbitrary"),
        name=name,
    )(*args)


def _rope(x, c_ref, s1_ref, s2_ref):
    c, s1, s2 = c_ref[...], s1_ref[...], s2_ref[...]
    out = []
    for b in range(x.shape[-1] // LANES):
        xb = x[:, b * LANES:(b + 1) * LANES]
        up = pltpu.roll(xb, LANES - 32, 1)
        dn = pltpu.roll(xb, 32, 1)
        out.append(xb * c + up * s1 + dn * s2)
    return jnp.concatenate(out, axis=1)


def _mm_kernel(*refs, rope_tiles, plain_scale):
    if rope_tiles is None:
        a_ref, w_ref, o_ref = refs
        o_ref[...] = jnp.dot(a_ref[...], w_ref[...], preferred_element_type=F32).astype(o_ref.dtype)
        return
    a_ref, w_ref, c_ref, s1_ref, s2_ref, o_ref = refs
    acc = jnp.dot(a_ref[...], w_ref[...], preferred_element_type=F32)
    j = pl.program_id(1)
    roped = (j >= rope_tiles[0]) & (j < rope_tiles[1])

    @pl.when(roped)
    def _():
        o_ref[...] = _rope(acc, c_ref, s1_ref, s2_ref).astype(o_ref.dtype)

    @pl.when(jnp.logical_not(roped))
    def _():
        o_ref[...] = (acc * plain_scale).astype(o_ref.dtype)


def _rope_table_specs(bm, bn, rope_q_cols, row_col):
    q0, q1 = (0, 0) if rope_q_cols is None else (rope_q_cols[0] // bn, rope_q_cols[1] // bn)

    def index(*grid):
        i, j = row_col(*grid)
        return i, jnp.where((j >= q0) & (j < q1), 1, 0)

    return [pl.BlockSpec((bm, LANES), index)] * 3


def _mm_call(a, w, *, out_dtype, rope=None, rope_cols=None, rope_q_cols=None, plain_scale=1.0, name="mm"):
    m, k = a.shape
    n = w.shape[-1]
    bm = _pick(m, (768, 640, 512, 384, 256, 128))
    bn = _pick(n, (512, 896, 256, 128))
    in_specs = [pl.BlockSpec((bm, k), lambda i, j: (i, 0)), pl.BlockSpec((k, bn), lambda i, j: (0, j))]
    args = [a, w]
    rope_tiles = None
    if rope is not None:
        rope_tiles = (rope_cols[0] // bn, rope_cols[1] // bn)
        in_specs += _rope_table_specs(bm, bn, rope_q_cols, lambda i, j: (i, j))
        args += list(rope)
    return pl.pallas_call(
        functools.partial(_mm_kernel, rope_tiles=rope_tiles, plain_scale=plain_scale),
        grid=(m // bm, n // bn),
        in_specs=in_specs,
        out_specs=pl.BlockSpec((bm, bn), lambda i, j: (i, j)),
        out_shape=jax.ShapeDtypeStruct((m, n), out_dtype),
        compiler_params=_params("arbitrary", "arbitrary"),
        name=name,
    )(*args)


def _mm_res_kernel(a_ref, w_ref, res_ref, gate_ref, o_ref, *, bm, n_lat):
    acc = jnp.dot(a_ref[...], w_ref[...], preferred_element_type=F32)
    gate = _sel(_is_ctx_rows(pl.program_id(0), bm, n_lat), gate_ref)
    o_ref[...] = res_ref[...] + gate * acc


def _mm_res_call(a, w, res, mod, gate_blk, *, n_lat, name):
    m, n = res.shape
    k = a.shape[-1]
    bm = _pick(m, (768, 640, 384, 256, 128))
    bn = _pick(n, (256, 128))
    return pl.pallas_call(
        functools.partial(_mm_res_kernel, bm=bm, n_lat=n_lat),
        grid=(m // bm, n // bn),
        in_specs=[pl.BlockSpec((bm, k), lambda i, j: (i, 0)),
                  pl.BlockSpec((k, bn), lambda i, j: (0, j)),
                  pl.BlockSpec((bm, bn), lambda i, j: (i, j)),
                  pl.BlockSpec((SUBLANES, bn), lambda i, j: (0, gate_blk * (n // bn) + j))],
        out_specs=pl.BlockSpec((bm, bn), lambda i, j: (i, j)),
        out_shape=jax.ShapeDtypeStruct((m, n), F32),
        compiler_params=_params("arbitrary", "arbitrary"),
        name=name,
    )(a, w, res, mod)


CAST_ROWS = 256


def _cast_weights(w_refs, wb_refs):
    for w_ref, wb_ref in zip(w_refs, wb_refs):
        nrows = math.gcd(w_ref.shape[0], CAST_ROWS)

        def step(r, _, w_ref=w_ref, wb_ref=wb_ref, nrows=nrows):
            rows = pl.ds(pl.multiple_of(r * nrows, nrows), nrows)
            wb_ref[rows, :] = w_ref[rows, :].astype(BF16)
            return 0
        lax.fori_loop(0, w_ref.shape[0] // nrows, step, 0)


def _wmm_kernel(*refs, parts, epi, bm, n_lat, rope_tiles, side):
    n_extra = {"plain": 0, "transpose": 0, "rope": 3, "res": 2, "swiglu": 0}[epi]
    n_w = 2 * parts if epi == "swiglu" else parts
    a_refs, w_refs = refs[:parts], refs[parts:parts + n_w]
    extra = refs[parts + n_w:parts + n_w + n_extra]
    rest = refs[parts + n_w + n_extra:]
    if side:
        side_in, o_ref, side_out = rest[:3]
        wb_refs = rest[3:]
        side_out[...] = side_in[...].astype(BF16)
    else:
        o_ref, wb_refs = rest[0], rest[1:]
    j, i = pl.program_id(0), pl.program_id(1)

    @pl.when(i == 0)
    def _():
        _cast_weights(w_refs, wb_refs)

    def mm(wbs):
        acc = jnp.dot(a_refs[0][...], wbs[0][...], preferred_element_type=F32)
        for a_ref, wb_ref in zip(a_refs[1:], wbs[1:]):
            acc = acc + jnp.dot(a_ref[...], wb_ref[...], preferred_element_type=F32)
        return acc

    if epi == "swiglu":
        g = mm(wb_refs[:parts])
        u = mm(wb_refs[parts:])
        o_ref[...] = (g * jax.nn.sigmoid(g) * u).astype(o_ref.dtype)
    elif epi == "res":
        res_ref, gate_ref = extra
        gate = _sel(_is_ctx_rows(i, bm, n_lat), gate_ref)
        o_ref[...] = res_ref[...] + gate * mm(wb_refs)
    elif epi == "rope":
        acc = mm(wb_refs)
        roped = (j >= rope_tiles[0]) & (j < rope_tiles[1])

        @pl.when(roped)
        def _():
            o_ref[...] = _rope(acc, *extra).astype(o_ref.dtype)

        @pl.when(jnp.logical_not(roped))
        def _():
            o_ref[...] = acc.astype(o_ref.dtype)
    elif epi == "transpose":
        o_ref[...] = mm(wb_refs).T.astype(o_ref.dtype)
    else:
        o_ref[...] = mm(wb_refs).astype(o_ref.dtype)


def _wmm_call(a_list, weights, *, n, col0=0, out_dtype, epi="plain", rope=None, rope_cols=None, rope_q_cols=None,
              res=None, mod=None,
              gate_blk=0, n_lat=0, bm_cands=(768, 640, 512, 384, 256, 128), bn_cands=(512, 256, 128),
              side_cast=None, name="wmm"):
    m = a_list[0].shape[0]
    parts = len(a_list)
    bm = _pick(m, bm_cands)
    bn = _pick(math.gcd(n, col0) if col0 else n, bn_cands)
    cb0 = col0 // bn
    in_specs = [pl.BlockSpec((bm, a.shape[1]), lambda j, i: (i, 0)) for a in a_list]
    args = list(a_list)
    scratch = []
    for idx, (w, layer, rb) in enumerate(weights):
        k = a_list[idx % parts].shape[1]
        if layer is None:
            in_specs.append(pl.BlockSpec((k, bn), lambda j, i, rb=rb: (rb, cb0 + j)))
        else:
            in_specs.append(pl.BlockSpec((None, k, bn), lambda j, i, layer=layer, rb=rb: (layer, rb, cb0 + j)))
        args.append(w)
        scratch.append(pltpu.VMEM((k, bn), BF16))
    rope_tiles = None
    if epi == "rope":
        rope_tiles = (rope_cols[0] // bn, rope_cols[1] // bn)
        in_specs += _rope_table_specs(bm, bn, rope_q_cols, lambda j, i: (i, j))
        args += list(rope)
    elif epi == "res":
        in_specs += [pl.BlockSpec((bm, bn), lambda j, i: (i, j)),
                     pl.BlockSpec((SUBLANES, bn), lambda j, i: (0, gate_blk * (n // bn) + j))]
        args += [res, mod]
    out_specs = (pl.BlockSpec((bn, bm), lambda j, i: (j, i)) if epi == "transpose"
                 else pl.BlockSpec((bm, bn), lambda j, i: (i, j)))
    out_shape = jax.ShapeDtypeStruct((n, m) if epi == "transpose" else (m, n), out_dtype)
    if side_cast is not None:
        w_side, side_layer = side_cast
        _, side_rows, side_cols = w_side.shape
        steps, ni = (n // bn) * (m // bm), m // bm
        slab = next(r for r in (64, 128, 256, 512, 1024, side_rows) if side_rows % r == 0 and side_rows // r <= steps)
        last = side_rows // slab - 1
        in_specs.append(pl.BlockSpec((None, slab, side_cols),
                                     lambda j, i: (side_layer, jnp.minimum(j * ni + i, last), 0)))
        args.append(w_side)
        out_specs = (out_specs, pl.BlockSpec((slab, side_cols), lambda j, i: (jnp.minimum(j * ni + i, last), 0)))
        out_shape = (out_shape, jax.ShapeDtypeStruct((side_rows, side_cols), BF16))
    return pl.pallas_call(
        functools.partial(_wmm_kernel, parts=parts, epi=epi, bm=bm, n_lat=n_lat, rope_tiles=rope_tiles,
                          side=side_cast is not None),
        grid=(n // bn, m // bm),
        in_specs=in_specs,
        out_specs=out_specs,
        out_shape=out_shape,
        scratch_shapes=scratch,
        compiler_params=_params("arbitrary", "arbitrary"),
        name=name,
    )(*args)


def _conv_kernel(bg_ref, cg_ref, xa_ref, cgp_ref, xap_ref, cgn_ref, xan_ref, w_ref, o_ref, *, bm, n_lat, m):
    i = pl.program_id(0)
    u = cg_ref[...] * xa_ref[...]
    u_prev = cgp_ref[SUBLANES - 1:SUBLANES, :] * xap_ref[SUBLANES - 1:SUBLANES, :]
    u_next = cgn_ref[0:1, :] * xan_ref[0:1, :]
    loc = lax.broadcasted_iota(jnp.int32, (bm, 1), 0)
    rows = i * bm + loc
    prev = jnp.where(loc == 0, u_prev, pltpu.roll(u, 1, 0))
    prev = jnp.where((rows == 0) | (rows == n_lat), 0.0, prev)
    nxt = jnp.where(loc == bm - 1, u_next, pltpu.roll(u, bm - 1, 0))
    nxt = jnp.where((rows == n_lat - 1) | (rows == m - 1), 0.0, nxt)
    y = w_ref[0:1, :] * prev + w_ref[1:2, :] * u + w_ref[2:3, :] * nxt
    o_ref[...] = (bg_ref[...] * y).astype(o_ref.dtype)


def _conv_call(pa, conv_w, *, n_lat):
    m = pa.shape[0]
    aw = conv_w.shape[-1]
    bm = _pick(math.gcd(n_lat, m - n_lat), (256, 128, 64, 32, 16, 8))
    bc = _pick(aw, (2048, 1024, 512, 256, 128))
    nc = aw // bc
    rb = bm // SUBLANES
    last = m // SUBLANES - 1
    main = lambda off: pl.BlockSpec((bm, bc), lambda i, c: (i, off * nc + c))
    prev = lambda off: pl.BlockSpec((SUBLANES, bc), lambda i, c: (jnp.maximum(i * rb - 1, 0), off * nc + c))
    nxt = lambda off: pl.BlockSpec((SUBLANES, bc), lambda i, c: (jnp.minimum((i + 1) * rb, last), off * nc + c))
    return pl.pallas_call(
        functools.partial(_conv_kernel, bm=bm, n_lat=n_lat, m=m),
        grid=(m // bm, nc),
        in_specs=[main(0), main(1), main(2), prev(1), prev(2), nxt(1), nxt(2),
                  pl.BlockSpec((conv_w.shape[0], bc), lambda i, c: (0, c))],
        out_specs=pl.BlockSpec((bm, bc), lambda i, c: (i, c)),
        out_shape=jax.ShapeDtypeStruct((m, aw), BF16),
        compiler_params=_params("arbitrary", "arbitrary"),
        name="short_conv",
    )(pa, pa, pa, pa, pa, pa, pa, conv_w)


def _pool_kernel(u_ref, up_ref, un_ref, w_ref, sc_ref, o_ref, *, bm, n_lat, m, pg):
    i = pl.program_id(0)
    ext = bm + 2 * POOL_HALO
    is_ctx = i * bm >= n_lat
    lo = jnp.where(is_ctx, n_lat, 0)
    hi = jnp.where(is_ctx, m, n_lat)
    rows_ext = i * bm - POOL_HALO + lax.broadcasted_iota(jnp.int32, (ext, 1), 0)
    valid = (rows_ext >= lo) & (rows_ext < hi)
    t = i * bm + lax.broadcasted_iota(jnp.int32, (bm, 1), 0) - lo
    t_len = hi - lo
    for g, win in enumerate(POOL_WINDOWS):
        cols = slice(g * pg, (g + 1) * pg)
        u = u_ref[:, cols]
        e = jnp.concatenate([up_ref[:, cols], u, un_ref[:, cols]], axis=0)
        e = jnp.where(valid, e, 0.0)
        span = 1
        while span < win:
            e = e + pltpu.roll(e, ext - span, 0)
            span *= 2
        start = POOL_HALO - win // 2
        if start:
            e = pltpu.roll(e, ext - start, 0)
        wsum = e[0:bm, :]
        w_lo = jnp.clip(t - win // 2, 0, t_len - 1)
        w_hi = jnp.clip(t + win // 2 - 1, 0, t_len - 1)
        cnt = (w_hi - w_lo + 1).astype(F32)
        diff = (wsum / cnt - u).astype(BF16)
        y = jnp.dot(diff, w_ref[g], preferred_element_type=F32) * sc_ref[:, cols]
        o_ref[:, cols] = y.astype(o_ref.dtype)


def _pool_call(u, pool_w, pool_scale, *, n_lat):
    m, pw = u.shape
    pg = pw // len(POOL_WINDOWS)
    bm = _pick(math.gcd(n_lat, m - n_lat), (256, 128, 64, 32, 16, 8))
    rb = bm // SUBLANES
    last = m // SUBLANES - 1
    return pl.pallas_call(
        functools.partial(_pool_kernel, bm=bm, n_lat=n_lat, m=m, pg=pg),
        grid=(m // bm,),
        in_specs=[pl.BlockSpec((bm, pw), lambda i: (i, 0)),
                  pl.BlockSpec((SUBLANES, pw), lambda i: (jnp.maximum(i * rb - 1, 0), 0)),
                  pl.BlockSpec((SUBLANES, pw), lambda i: (jnp.minimum((i + 1) * rb, last), 0)),
                  pl.BlockSpec(pool_w.shape, lambda i: (0, 0, 0)),
                  pl.BlockSpec((1, pw), lambda i: (0, 0))],
        out_specs=pl.BlockSpec((bm, pw), lambda i: (i, 0)),
        out_shape=jax.ShapeDtypeStruct((m, pw), BF16),
        compiler_params=_params("arbitrary"),
        name="multiscale_pool",
    )(u, u, u, pool_w, pool_scale.reshape(1, pw))


def _flash_tiles(q_tiles, k_chunk, vt_ref, finish, *, nk, bk):
    dv = vt_ref.shape[0]

    def update(t, s, state):
        m_run, l_run, acc = state
        m_new = jnp.maximum(m_run, jnp.max(s, axis=0, keepdims=True))
        alpha = jnp.exp2(m_run - m_new)
        p = jnp.exp2(s - m_new)
        l_new = alpha * l_run + jnp.sum(p, axis=0, keepdims=True)
        acc = alpha * acc + jnp.dot(vt_ref[:, t * bk:(t + 1) * bk], p.astype(BF16), preferred_element_type=F32)
        return m_new, l_new, acc

    pending = None
    for n, qs in enumerate(q_tiles):
        r = qs.shape[0]
        scores_of = lambda t, qs=qs: lax.dot_general(k_chunk(t), qs, (((1,), (1,)), ((), ())),
                                                     preferred_element_type=F32)
        state = (jnp.full((1, r), -jnp.inf, F32), jnp.zeros((1, r), F32), jnp.zeros((dv, r), F32))
        scores = scores_of(0)
        if pending is not None:
            pending()
        for t in range(nk - 1):
            nxt = scores_of(t + 1)
            state = update(t, scores, state)
            scores = nxt

        def pending(n=n, scores=scores, state=state):
            _, l_run, acc = update(nk - 1, scores, state)
            finish(n, l_run, acc)
    pending()


def _diff_attn_kernel(q_ref, k_ref, vt_ref, lq1_ref, lk1_ref, lq2_ref, lk2_ref, g_ref, *rest, bq, tiles, bk, nk,
                      lam_init):
    o_ref = rest[-1]
    bt = bq // tiles
    lane = lax.broadcasted_iota(jnp.int32, (bt, HEAD_DIM), 1)
    zero = jnp.zeros((bt, HEAD_DIM), q_ref.dtype)
    q_tiles = []
    for n in range(tiles):
        q = q_ref[n * bt:(n + 1) * bt, :]
        q_tiles.append(jnp.concatenate([jnp.where(lane < HEAD_DIM // 2, q, zero),
                                        jnp.where(lane >= HEAD_DIM // 2, q, zero)], axis=0))
    lam = (jnp.exp(jnp.sum(lq1_ref[...] * lk1_ref[...], axis=1, keepdims=True))
           - jnp.exp(jnp.sum(lq2_ref[...] * lk2_ref[...], axis=1, keepdims=True)) + lam_init)

    def finish(n, l_run, acc):
        o_t = acc / l_run
        o = (o_t[:, :bt] - lam * o_t[:, bt:]).T
        y = o * lax.rsqrt(jnp.mean(o * o, axis=-1, keepdims=True) + EPS)
        o_ref[n * bt:(n + 1) * bt, :] = (y * g_ref[...] * (1 - lam_init)).astype(o_ref.dtype)

    _flash_tiles(q_tiles, lambda t: k_ref[t * bk:(t + 1) * bk, :], vt_ref, finish, nk=nk, bk=bk)


def _out_alias(out, n_in):
    return jax.ShapeDtypeStruct(out.shape, out.dtype), [pl.BlockSpec(memory_space=pl.ANY)], [out], {n_in: 0}


def _diff_attn_call(qk, vt, lam_vecs, subln, out, *, heads, q_rows, q_row0, kv_rows, kv_row0, lam_init, name):
    bq = _pick(q_rows, (512, 256, 128))
    tiles = max(1, bq // 256)
    bk = _pick(kv_rows, (2816, 768, 512, 640, 384, 256, 128))
    qb0, kb0 = q_row0 // bq, kv_row0 // kv_rows
    vec = pl.BlockSpec((1, HEAD_DIM // 2), lambda h, i: (0, 0))
    out_shape, alias_specs, alias_args, aliases = _out_alias(out, 8)
    return pl.pallas_call(
        functools.partial(_diff_attn_kernel, bq=bq, tiles=tiles, bk=bk, nk=kv_rows // bk, lam_init=lam_init),
        grid=(heads, q_rows // bq),
        in_specs=[pl.BlockSpec((bq, HEAD_DIM), lambda h, i: (qb0 + i, h)),
                  pl.BlockSpec((kv_rows, HEAD_DIM), lambda h, i: (kb0, heads + h)),
                  pl.BlockSpec((HEAD_DIM, kv_rows), lambda h, i: (h, kb0)),
                  vec, vec, vec, vec,
                  pl.BlockSpec((1, HEAD_DIM), lambda h, i: (0, 0))] + alias_specs,
        out_specs=pl.BlockSpec((bq, HEAD_DIM), lambda h, i: (qb0 + i, h)),
        out_shape=out_shape,
        input_output_aliases=aliases,
        compiler_params=_params("arbitrary", "arbitrary"),
        name=name,
    )(qk, qk, vt, *lam_vecs, subln, *alias_args)


def _mla_attn_kernel(qn_ref, qp_ref, kn_ref, kp_ref, vt_ref, *rest, bq, tiles, bk, nk):
    o_ref = rest[-1]
    bt = bq // tiles
    q_tiles = [jnp.concatenate([qn_ref[n * bt:(n + 1) * bt, :], qp_ref[n * bt:(n + 1) * bt, :]], axis=1)
               for n in range(tiles)]

    def k_chunk(t):
        rows = slice(t * bk, (t + 1) * bk)
        return jnp.concatenate([kn_ref[rows, :], kp_ref[rows, :]], axis=1)

    def finish(n, l_run, acc):
        o_ref[n * bt:(n + 1) * bt, :] = (acc / l_run).T.astype(o_ref.dtype)

    _flash_tiles(q_tiles, k_chunk, vt_ref, finish, nk=nk, bk=bk)


def _mla_attn_call(q, kn, kpe, vt, out, *, heads, q_rows, q_row0, kv_rows, kv_row0, name):
    bq = _pick(q_rows, (1024, 512, 256, 128))
    tiles = max(1, bq // 512)
    bk = _pick(kv_rows, (2816, 768, 512, 640, 384, 256, 128))
    qb0, kb0 = q_row0 // bq, kv_row0 // kv_rows
    out_shape, alias_specs, alias_args, aliases = _out_alias(out, 5)
    return pl.pallas_call(
        functools.partial(_mla_attn_kernel, bq=bq, tiles=tiles, bk=bk, nk=kv_rows // bk),
        grid=(heads, q_rows // bq),
        in_specs=[pl.BlockSpec((bq, HEAD_DIM), lambda h, i: (qb0 + i, h)),
                  pl.BlockSpec((bq, HEAD_DIM), lambda h, i: (qb0 + i, heads + h)),
                  pl.BlockSpec((kv_rows, HEAD_DIM), lambda h, i: (kb0, h)),
                  pl.BlockSpec((kv_rows, LANES), lambda h, i: (kb0, 0)),
                  pl.BlockSpec((HEAD_DIM, kv_rows), lambda h, i: (h, kb0))] + alias_specs,
        out_specs=pl.BlockSpec((bq, HEAD_DIM), lambda h, i: (qb0 + i, h)),
        out_shape=out_shape,
        input_output_aliases=aliases,
        compiler_params=_params("arbitrary", "arbitrary"),
        name=name,
    )(q, q, kn, kpe, vt, *alias_args)


def _rope_tables(seq, ctx_len, q_scale):
    rows = seq // GRID_W
    row = jnp.broadcast_to(jnp.arange(rows, dtype=F32)[:, None], (rows, GRID_W)).reshape(seq)
    col = jnp.broadcast_to(jnp.arange(GRID_W, dtype=F32)[None, :], (rows, GRID_W)).reshape(seq)
    n_freq = 16
    inv_freq = ROPE_BASE ** (-jnp.arange(n_freq, dtype=F32) / n_freq)
    ang = jnp.concatenate([row[:, None] * inv_freq, col[:, None] * inv_freq], axis=-1)
    cos, sin = jnp.cos(ang), jnp.sin(ang)
    zero = jnp.zeros_like(sin)
    c = jnp.concatenate([cos, cos, cos, cos], axis=-1)
    s1 = jnp.concatenate([-sin, zero, -sin, zero], axis=-1)
    s2 = jnp.concatenate([zero, sin, zero, sin], axis=-1)
    pad = lambda t, v: jnp.concatenate([t, jnp.full((ctx_len, LANES), v, F32)], axis=0)
    return tuple(jnp.concatenate([t, t * q_scale], axis=1) for t in (pad(c, 1.0), pad(s1, 0.0), pad(s2, 0.0)))


def kernel(x, c, ctx, c_ctx, ada_down, ada_up, ada_bias, norm_mix, norm_ffn, ffn_gate, ffn_up, ffn_down, ev_w_in, ev_conv, ev_lam_q1, ev_lam_k1, ev_lam_q2, ev_lam_k2, ev_subln, ev_w_out, od_w_in, od_q_norm, od_w_uq, od_kv_norm, od_w_ukv, od_pool_w, od_pool_scale, od_w_out, final_norm):
    _, seq, d = x.shape
    ctx_len = ctx.shape[1]
    m = seq + ctx_len
    depth = ada_down.shape[0]
    half = d // 2
    heads = half // HEAD_DIM
    q_rank = od_q_norm.shape[-1]
    kv_rank = od_kv_norm.shape[-1]
    rope_dim = HEAD_DIM // 2
    bf = lambda t: t.astype(BF16)

    r = jnp.concatenate([x[0], ctx[0]], axis=0)
    diff_q_scale = (HEAD_DIM // 2) ** -0.5 * LOG2E
    mla_q_scale = (HEAD_DIM + rope_dim) ** -0.5 * LOG2E
    tables_diff = _rope_tables(seq, ctx_len, diff_q_scale)
    tables_mla = _rope_tables(seq, ctx_len, mla_q_scale)

    cc = jnp.zeros((SUBLANES, d), F32).at[0].set(c[0]).at[1].set(c_ctx)
    mods = _ada_call(cc, ada_down, ada_up, ada_bias.reshape(depth, 1, -1))

    for l in range(depth):
        mod = mods[l]
        i = l // 2
        h = _norm_call(r, norm_mix[l], rows=m, out_dtype=BF16, mod=mod, shift_blk=0, scale_blk=1, n_lat=seq,
                       name="norm_mod_mix")
        if l % 2 == 0:
            lam_init = 0.8 - 0.6 * math.exp(-0.3 * l)
            big_rows = (1408, 1280, 768, 640, 512, 256, 128)
            pa = _wmm_call([h], [(ev_w_in, i, 0)], n=3 * half, out_dtype=F32, bm_cands=big_rows, name="ev_in_conv")
            qk = _wmm_call([h], [(ev_w_in, i, 0)], n=2 * half, col0=3 * half, out_dtype=BF16, epi="rope",
                           rope=tables_diff, rope_cols=(0, 2 * half), rope_q_cols=(0, half), bm_cands=big_rows,
                           name="ev_in_qk")
            vt = _wmm_call([h], [(ev_w_in, i, 0)], n=half, col0=5 * half, out_dtype=BF16, epi="transpose",
                           bm_cands=big_rows, name="ev_in_vt")
            ya = _conv_call(pa, ev_conv[i], n_lat=seq)
            lam_vecs = [v[i].reshape(1, -1) for v in (ev_lam_q1, ev_lam_k1, ev_lam_q2, ev_lam_k2)]
            subln = ev_subln[i].reshape(1, -1)
            yb = jnp.zeros((m, half), BF16)
            yb = _diff_attn_call(qk, vt, lam_vecs, subln, yb, heads=heads, q_rows=seq, q_row0=0, kv_rows=m,
                                 kv_row0=0, lam_init=lam_init, name="diff_attn")
            yb = _diff_attn_call(qk, vt, lam_vecs, subln, yb, heads=heads, q_rows=ctx_len, q_row0=seq,
                                 kv_rows=ctx_len, kv_row0=seq, lam_init=lam_init, name="diff_attn_ctx")
            y1, y2 = ya, yb
            w_out = ev_w_out
        else:
            w_in = od_w_in[i]
            w_kpe = jnp.pad(w_in[:, q_rank + kv_rank:q_rank + kv_rank + rope_dim], ((0, 0), (0, LANES - rope_dim)))
            cq = _mm_call(h, bf(w_in[:, :q_rank]), out_dtype=F32, name="od_in_cq")
            ckv = _mm_call(h, bf(w_in[:, q_rank:q_rank + kv_rank]), out_dtype=F32, name="od_in_ckv")
            kpe = _mm_call(h, bf(w_kpe), out_dtype=BF16, rope=tables_mla, rope_cols=(0, LANES), name="od_in_kpe")
            u = _mm_call(h, bf(w_in[:, q_rank + kv_rank + rope_dim:]), out_dtype=F32, name="od_in_pool")
            cqn = _norm_call(cq, od_q_norm[i], rows=m, out_dtype=BF16, name="norm_cq")
            ckvn = _norm_call(ckv, od_kv_norm[i], rows=m, out_dtype=BF16, name="norm_ckv")
            w_uq = od_w_uq[i].reshape(q_rank, heads, HEAD_DIM + rope_dim)
            w_qn = w_uq[:, :, :HEAD_DIM].reshape(q_rank, half)
            w_qp = jnp.pad(w_uq[:, :, HEAD_DIM:], ((0, 0), (0, 0), (0, LANES - rope_dim))).reshape(q_rank, heads * LANES)
            q = _mm_call(cqn, bf(jnp.concatenate([w_qn, w_qp], axis=1)), out_dtype=BF16, rope=tables_mla,
                         rope_cols=(half, half + heads * LANES), rope_q_cols=(half, half + heads * LANES),
                         plain_scale=mla_q_scale, name="od_uq")
            w_ukv = od_w_ukv[i].reshape(kv_rank, heads, 2 * HEAD_DIM)
            kn = _mm_call(ckvn, bf(w_ukv[:, :, :HEAD_DIM].reshape(kv_rank, half)), out_dtype=BF16, name="od_uk")
            vt = _wmm_call([ckvn], [(w_ukv[:, :, HEAD_DIM:].reshape(kv_rank, half), None, 0)], n=half, out_dtype=BF16,
                           epi="transpose", name="od_uvt")
            ym = jnp.zeros((m, half), BF16)
            ym = _mla_attn_call(q, kn, kpe, vt, ym, heads=heads, q_rows=seq, q_row0=0, kv_rows=m, kv_row0=0,
                                name="mla_attn")
            ym = _mla_attn_call(q, kn, kpe, vt, ym, heads=heads, q_rows=ctx_len, q_row0=seq, kv_rows=ctx_len,
                                kv_row0=seq, name="mla_attn_ctx")
            y1 = _pool_call(u, bf(od_pool_w[i]), od_pool_scale[i], n_lat=seq)
            y2 = ym
            w_out = od_w_out
        r = _wmm_call([y1, y2], [(w_out, i, 0), (w_out, i, 1)], n=d, out_dtype=F32, epi="res", res=r, mod=mod,
                      gate_blk=2, n_lat=seq, name="mix_out")
        h2 = _norm_call(r, norm_ffn[l], rows=m, out_dtype=BF16, mod=mod, shift_blk=3, scale_blk=4, n_lat=seq,
                        name="norm_mod_ffn")
        a, w_down = _wmm_call([h2], [(ffn_gate, l, 0), (ffn_up, l, 0)], n=ffn_gate.shape[-1], out_dtype=BF16,
                              epi="swiglu", bm_cands=(1408, 1280, 768, 640, 512, 256, 128), bn_cands=(256, 128),
                              side_cast=(ffn_down, l), name="ffn_gate_up")
        r = _mm_res_call(a, w_down, r, mod, 5, n_lat=seq, name="ffn_down")
    out = _norm_call(r, final_norm, rows=seq, out_dtype=F32, name="final_norm")
    return out[None]
```

```python
import functools
import math

import jax
import jax.numpy as jnp
from jax import lax
from jax.experimental import pallas as pl
from jax.experimental.pallas import tpu as pltpu

F32 = jnp.float32
BF16 = jnp.bfloat16

LANES = 128
SUBLANES = 8
BF16_ROWS = 16
HEAD_DIM = 128
GRID_W = 64
ROPE_BASE = 10000.0
EPS = 1e-6
POOL_WINDOWS = (2, 4, 8, 16)
POOL_HALO = 8
LOG2E = 1.4426950408889634
VMEM_LIMIT = 56 * 1024 * 1024


def _pick(n, cands):
    for c in cands:
        if c <= n and n % c == 0:
            return c
    return n


def _params(*sem):
    return pltpu.CompilerParams(dimension_semantics=sem, vmem_limit_bytes=VMEM_LIMIT)


def _is_ctx_rows(tile, bm, n_lat):
    rows = tile * bm + lax.broadcasted_iota(jnp.int32, (bm, 1), 0)
    return rows >= n_lat


def _sel(is_ctx, ref):
    return jnp.where(is_ctx, ref[1:2, :], ref[0:1, :])


def _ada_kernel(cc_ref, down_ref, up_ref, bias_ref, o_ref, t_ref):
    @pl.when(pl.program_id(1) == 0)
    def _():
        c = cc_ref[...]
        s = c * jax.nn.sigmoid(c)
        t = jnp.dot(s.astype(BF16), down_ref[...].astype(BF16), preferred_element_type=F32)
        t_ref[...] = t.astype(BF16)

    o_ref[...] = jnp.dot(t_ref[...], up_ref[...].astype(BF16), preferred_element_type=F32) + bias_ref[...]


def _ada_call(cc, down, up, bias):
    depth, d, rank = down.shape
    n = up.shape[-1]
    bn = _pick(n, (2048, 1024, 512, 256, 128))
    return pl.pallas_call(
        _ada_kernel,
        grid=(depth, n // bn),
        in_specs=[
            pl.BlockSpec((SUBLANES, d), lambda l, j: (0, 0)),
            pl.BlockSpec((None, d, rank), lambda l, j: (l, 0, 0)),
            pl.BlockSpec((None, rank, bn), lambda l, j: (l, 0, j)),
            pl.BlockSpec((None, 1, bn), lambda l, j: (l, 0, j)),
        ],
        out_specs=pl.BlockSpec((None, SUBLANES, bn), lambda l, j: (l, 0, j)),
        out_shape=jax.ShapeDtypeStruct((depth, SUBLANES, n), F32),
        scratch_shapes=[pltpu.VMEM((SUBLANES, rank), BF16)],
        compiler_params=_params("arbitrary", "arbitrary"),
        name="ada_mod",
    )(cc, down, up, bias)


NORM_ROWS = 16


def _norm_kernel(*refs, modulate, bm, n_lat):
    if modulate:
        x_ref, g_ref, sh_ref, sc_ref, o_ref = refs
    else:
        x_ref, g_ref, o_ref = refs
    g = g_ref[...]
    row0 = pl.program_id(0) * bm

    def step(r, _):
        rows = pl.ds(pl.multiple_of(r * NORM_ROWS, NORM_ROWS), NORM_ROWS)
        x = x_ref[rows, :].astype(F32)
        y = x * lax.rsqrt(jnp.mean(x * x, axis=-1, keepdims=True) + EPS)
        y = y * g
        if modulate:
            is_ctx = row0 + r * NORM_ROWS >= n_lat
            y = y * (1 + _sel(is_ctx, sc_ref)) + _sel(is_ctx, sh_ref)
        o_ref[rows, :] = y.astype(o_ref.dtype)
        return 0

    lax.fori_loop(0, bm // NORM_ROWS, step, 0, unroll=4)


def _norm_call(x, g, *, rows, out_dtype, mod=None, shift_blk=0, scale_blk=0, n_lat=0, name="rmsnorm"):
    k = x.shape[-1]
    bm = _pick(rows, (384, 320, 256, 128, 64, 32, 16, 8))
    in_specs = [pl.BlockSpec((bm, k), lambda i: (i, 0)), pl.BlockSpec((1, k), lambda i: (0, 0))]
    args = [x, g.reshape(1, k)]
    if mod is not None:
        in_specs += [pl.BlockSpec((SUBLANES, k), lambda i: (0, shift_blk)),
                     pl.BlockSpec((SUBLANES, k), lambda i: (0, scale_blk))]
        args += [mod, mod]
    return pl.pallas_call(
        functools.partial(_norm_kernel, modulate=mod is not None, bm=bm, n_lat=n_lat),
        grid=(rows // bm,),
        in_specs=in_specs,
        out_specs=pl.BlockSpec((bm, k), lambda i: (i, 0)),
        out_shape=jax.ShapeDtypeStruct((rows, k), out_dtype),
        compiler_params=_params("arbitrary"),
        name=name,
    )(*args)


def _rope(x, c, s1, s2):
    out = []
    for b in range(x.shape[-1] // LANES):
        xb = x[:, b * LANES:(b + 1) * LANES]
        up = pltpu.roll(xb, LANES - 32, 1)
        dn = pltpu.roll(xb, 32, 1)
        out.append(xb * c + up * s1 + dn * s2)
    return jnp.concatenate(out, axis=1)


def _rope_table_specs(bm, bn, rope_q_cols, row_col):
    q0, q1 = (0, 0) if rope_q_cols is None else (rope_q_cols[0] // bn, rope_q_cols[1] // bn)

    def index(*grid):
        i, j = row_col(*grid)
        return i, jnp.where((j >= q0) & (j < q1), 1, 0)

    return [pl.BlockSpec((bm, LANES), index)] * 3


def _mm_kernel(a_ref, w_ref, o_ref):
    o_ref[...] = jnp.dot(a_ref[...], w_ref[...], preferred_element_type=F32).astype(o_ref.dtype)


def _mm_call(a, w, *, out_dtype, name):
    m, k = a.shape
    n = w.shape[-1]
    bm = _pick(m, (768, 640, 512, 384, 256, 128))
    bn = _pick(n, (512, 896, 256, 128))
    return pl.pallas_call(
        _mm_kernel,
        grid=(m // bm, n // bn),
        in_specs=[pl.BlockSpec((bm, k), lambda i, j: (i, 0)), pl.BlockSpec((k, bn), lambda i, j: (0, j))],
        out_specs=pl.BlockSpec((bm, bn), lambda i, j: (i, j)),
        out_shape=jax.ShapeDtypeStruct((m, n), out_dtype),
        compiler_params=_params("arbitrary", "arbitrary"),
        name=name,
    )(a, w)


def _mm_res_kernel(a_ref, w_ref, res_ref, gate_ref, o_ref, *, bm, n_lat):
    acc = jnp.dot(a_ref[...], w_ref[...], preferred_element_type=F32)
    gate = _sel(_is_ctx_rows(pl.program_id(0), bm, n_lat), gate_ref)
    o_ref[...] = res_ref[...] + gate * acc


def _mm_res_call(a, w, res, mod, gate_blk, *, n_lat, name):
    m, n = res.shape
    k = a.shape[-1]
    bm = _pick(m, (768, 640, 384, 256, 128))
    bn = _pick(n, (256, 128))
    return pl.pallas_call(
        functools.partial(_mm_res_kernel, bm=bm, n_lat=n_lat),
        grid=(m // bm, n // bn),
        in_specs=[pl.BlockSpec((bm, k), lambda i, j: (i, 0)),
                  pl.BlockSpec((k, bn), lambda i, j: (0, j)),
                  pl.BlockSpec((bm, bn), lambda i, j: (i, j)),
                  pl.BlockSpec((SUBLANES, bn), lambda i, j: (0, gate_blk * (n // bn) + j))],
        out_specs=pl.BlockSpec((bm, bn), lambda i, j: (i, j)),
        out_shape=jax.ShapeDtypeStruct((m, n), F32),
        compiler_params=_params("arbitrary", "arbitrary"),
        name=name,
    )(a, w, res, mod)


CAST_ROWS = 256


def _cast_weights(w_refs, wb_refs):
    for w_ref, wb_ref in zip(w_refs, wb_refs):
        nrows = math.gcd(w_ref.shape[0], CAST_ROWS)

        def step(r, _, w_ref=w_ref, wb_ref=wb_ref, nrows=nrows):
            rows = pl.ds(pl.multiple_of(r * nrows, nrows), nrows)
            wb_ref[rows, :] = w_ref[rows, :].astype(BF16)
            return 0
        lax.fori_loop(0, w_ref.shape[0] // nrows, step, 0)


ROPE_ROW_CHUNKS = 4


def _wmm_kernel(*refs, parts, epi, bm, n_lat, rope_tiles, rope_all, side, out_scale):
    n_extra = {"plain": 0, "transpose": 0, "rope": 3, "res": 2, "swiglu": 0}[epi]
    n_w = 2 * parts if epi == "swiglu" else parts
    a_refs, w_refs = refs[:parts], refs[parts:parts + n_w]
    extra = refs[parts + n_w:parts + n_w + n_extra]
    rest = refs[parts + n_w + n_extra:]
    if side:
        side_in, o_ref, side_out = rest[:3]
        wb_refs = rest[3:]
        side_out[...] = side_in[...].astype(BF16)
    else:
        o_ref, wb_refs = rest[0], rest[1:]
    j, i = pl.program_id(0), pl.program_id(1)

    @pl.when(i == 0)
    def _():
        _cast_weights(w_refs, wb_refs)

    def mm(wbs, rows=slice(None)):
        acc = jnp.dot(a_refs[0][rows, :], wbs[0][...], preferred_element_type=F32)
        for a_ref, wb_ref in zip(a_refs[1:], wbs[1:]):
            acc = acc + jnp.dot(a_ref[rows, :], wb_ref[...], preferred_element_type=F32)
        return acc

    if epi == "swiglu":
        g = mm(wb_refs[:parts])
        u = mm(wb_refs[parts:])
        o_ref[...] = (g * jax.nn.sigmoid(g) * u).astype(o_ref.dtype)
    elif epi == "res":
        res_ref, gate_ref = extra
        gate = _sel(_is_ctx_rows(i, bm, n_lat), gate_ref)
        o_ref[...] = res_ref[...] + gate * mm(wb_refs)
    elif epi == "rope" and rope_all:
        rc = bm // ROPE_ROW_CHUNKS
        pending = None
        for c in range(ROPE_ROW_CHUNKS):
            rows = slice(c * rc, (c + 1) * rc)
            acc = mm(wb_refs, rows)
            if pending is not None:
                pending()

            def pending(rows=rows, acc=acc):
                o_ref[rows, :] = _rope(acc, *(t[rows, :] for t in extra)).astype(o_ref.dtype)
        pending()
    elif epi == "rope":
        acc = mm(wb_refs)
        roped = (j >= rope_tiles[0]) & (j < rope_tiles[1])

        @pl.when(roped)
        def _():
            o_ref[...] = _rope(acc, *(t[...] for t in extra)).astype(o_ref.dtype)

        @pl.when(jnp.logical_not(roped))
        def _():
            o_ref[...] = acc.astype(o_ref.dtype)
    elif epi == "transpose":
        o_ref[...] = mm(wb_refs).T.astype(o_ref.dtype)
    else:
        o_ref[...] = (mm(wb_refs) * out_scale).astype(o_ref.dtype)


def _wmm_call(a_list, weights, *, n, col0=0, out_dtype, epi="plain", rope=None, rope_cols=None, rope_q_cols=None,
              res=None, mod=None,
              gate_blk=0, n_lat=0, bm_cands=(768, 640, 512, 384, 256, 128), bn_cands=(512, 256, 128),
              side_cast=None, out_scale=1.0, name="wmm"):
    m = a_list[0].shape[0]
    parts = len(a_list)
    bm = _pick(m, bm_cands)
    bn = _pick(math.gcd(n, col0) if col0 else n, bn_cands)
    cb0 = col0 // bn
    in_specs = [pl.BlockSpec((bm, a.shape[1]), lambda j, i: (i, 0)) for a in a_list]
    args = list(a_list)
    scratch = []
    for idx, (w, layer, rb) in enumerate(weights):
        k = a_list[idx % parts].shape[1]
        if layer is None:
            in_specs.append(pl.BlockSpec((k, bn), lambda j, i, rb=rb: (rb, cb0 + j)))
        else:
            in_specs.append(pl.BlockSpec((None, k, bn), lambda j, i, layer=layer, rb=rb: (layer, rb, cb0 + j)))
        args.append(w)
        scratch.append(pltpu.VMEM((k, bn), BF16))
    rope_tiles, rope_all = None, False
    if epi == "rope":
        rope_tiles = (rope_cols[0] // bn, rope_cols[1] // bn)
        rope_all = rope_tiles == (0, n // bn) and bm % (ROPE_ROW_CHUNKS * BF16_ROWS) == 0
        in_specs += _rope_table_specs(bm, bn, rope_q_cols, lambda j, i: (i, j))
        args += list(rope)
    elif epi == "res":
        in_specs += [pl.BlockSpec((bm, bn), lambda j, i: (i, j)),
                     pl.BlockSpec((SUBLANES, bn), lambda j, i: (0, gate_blk * (n // bn) + j))]
        args += [res, mod]
    out_specs = (pl.BlockSpec((bn, bm), lambda j, i: (j, i)) if epi == "transpose"
                 else pl.BlockSpec((bm, bn), lambda j, i: (i, j)))
    out_shape = jax.ShapeDtypeStruct((n, m) if epi == "transpose" else (m, n), out_dtype)
    if side_cast is not None:
        w_side, side_layer = side_cast
        _, side_rows, side_cols = w_side.shape
        steps, ni = (n // bn) * (m // bm), m // bm
        slab = next(r for r in (64, 128, 256, 512, 1024, side_rows) if side_rows % r == 0 and side_rows // r <= steps)
        last = side_rows // slab - 1
        in_specs.append(pl.BlockSpec((None, slab, side_cols),
                                     lambda j, i: (side_layer, jnp.minimum(j * ni + i, last), 0)))
        args.append(w_side)
        out_specs = (out_specs, pl.BlockSpec((slab, side_cols), lambda j, i: (jnp.minimum(j * ni + i, last), 0)))
        out_shape = (out_shape, jax.ShapeDtypeStruct((side_rows, side_cols), BF16))
    return pl.pallas_call(
        functools.partial(_wmm_kernel, parts=parts, epi=epi, bm=bm, n_lat=n_lat, rope_tiles=rope_tiles,
                          rope_all=rope_all, side=side_cast is not None, out_scale=out_scale),
        grid=(n // bn, m // bm),
        in_specs=in_specs,
        out_specs=out_specs,
        out_shape=out_shape,
        scratch_shapes=scratch,
        compiler_params=_params("arbitrary", "arbitrary"),
        name=name,
    )(*args)


def _conv_kernel(bg_ref, cg_ref, xa_ref, cgp_ref, xap_ref, cgn_ref, xan_ref, w_ref, o_ref, *, bm, n_lat, m):
    i = pl.program_id(0)
    u = cg_ref[...] * xa_ref[...]
    u_prev = cgp_ref[SUBLANES - 1:SUBLANES, :] * xap_ref[SUBLANES - 1:SUBLANES, :]
    u_next = cgn_ref[0:1, :] * xan_ref[0:1, :]
    loc = lax.broadcasted_iota(jnp.int32, (bm, 1), 0)
    rows = i * bm + loc
    prev = jnp.where(loc == 0, u_prev, pltpu.roll(u, 1, 0))
    prev = jnp.where((rows == 0) | (rows == n_lat), 0.0, prev)
    nxt = jnp.where(loc == bm - 1, u_next, pltpu.roll(u, bm - 1, 0))
    nxt = jnp.where((rows == n_lat - 1) | (rows == m - 1), 0.0, nxt)
    y = w_ref[0:1, :] * prev + w_ref[1:2, :] * u + w_ref[2:3, :] * nxt
    o_ref[...] = (bg_ref[...] * y).astype(o_ref.dtype)


def _conv_call(pa, conv_w, *, n_lat):
    m = pa.shape[0]
    aw = conv_w.shape[-1]
    bm = _pick(math.gcd(n_lat, m - n_lat), (256, 128, 64, 32, 16, 8))
    bc = _pick(aw, (2048, 1024, 512, 256, 128))
    nc = aw // bc
    rb = bm // SUBLANES
    last = m // SUBLANES - 1
    main = lambda off: pl.BlockSpec((bm, bc), lambda i, c: (i, off * nc + c))
    prev = lambda off: pl.BlockSpec((SUBLANES, bc), lambda i, c: (jnp.maximum(i * rb - 1, 0), off * nc + c))
    nxt = lambda off: pl.BlockSpec((SUBLANES, bc), lambda i, c: (jnp.minimum((i + 1) * rb, last), off * nc + c))
    return pl.pallas_call(
        functools.partial(_conv_kernel, bm=bm, n_lat=n_lat, m=m),
        grid=(m // bm, nc),
        in_specs=[main(0), main(1), main(2), prev(1), prev(2), nxt(1), nxt(2),
                  pl.BlockSpec((conv_w.shape[0], bc), lambda i, c: (0, c))],
        out_specs=pl.BlockSpec((bm, bc), lambda i, c: (i, c)),
        out_shape=jax.ShapeDtypeStruct((m, aw), BF16),
        compiler_params=_params("arbitrary", "arbitrary"),
        name="short_conv",
    )(pa, pa, pa, pa, pa, pa, pa, conv_w)


def _pool_kernel(u_ref, up_ref, un_ref, w_ref, sc_ref, o_ref, *, bm, n_lat, m, pg):
    i = pl.program_id(0)
    ext = bm + 2 * POOL_HALO
    is_ctx = i * bm >= n_lat
    lo = jnp.where(is_ctx, n_lat, 0)
    hi = jnp.where(is_ctx, m, n_lat)
    rows_ext = i * bm - POOL_HALO + lax.broadcasted_iota(jnp.int32, (ext, 1), 0)
    valid = (rows_ext >= lo) & (rows_ext < hi)
    t = i * bm + lax.broadcasted_iota(jnp.int32, (bm, 1), 0) - lo
    t_len = hi - lo
    for g, win in enumerate(POOL_WINDOWS):
        cols = slice(g * pg, (g + 1) * pg)
        u = u_ref[:, cols]
        e = jnp.concatenate([up_ref[:, cols], u, un_ref[:, cols]], axis=0)
        e = jnp.where(valid, e, 0.0)
        span = 1
        while span < win:
            e = e + pltpu.roll(e, ext - span, 0)
            span *= 2
        start = POOL_HALO - win // 2
        if start:
            e = pltpu.roll(e, ext - start, 0)
        wsum = e[0:bm, :]
        w_lo = jnp.clip(t - win // 2, 0, t_len - 1)
        w_hi = jnp.clip(t + win // 2 - 1, 0, t_len - 1)
        cnt = (w_hi - w_lo + 1).astype(F32)
        diff = (wsum / cnt - u).astype(BF16)
        y = jnp.dot(diff, w_ref[g], preferred_element_type=F32) * sc_ref[:, cols]
        o_ref[:, cols] = y.astype(o_ref.dtype)


def _pool_call(u, pool_w, pool_scale, *, n_lat):
    m, pw = u.shape
    pg = pw // len(POOL_WINDOWS)
    bm = _pick(math.gcd(n_lat, m - n_lat), (256, 128, 64, 32, 16, 8))
    rb = bm // SUBLANES
    last = m // SUBLANES - 1
    return pl.pallas_call(
        functools.partial(_pool_kernel, bm=bm, n_lat=n_lat, m=m, pg=pg),
        grid=(m // bm,),
        in_specs=[pl.BlockSpec((bm, pw), lambda i: (i, 0)),
                  pl.BlockSpec((SUBLANES, pw), lambda i: (jnp.maximum(i * rb - 1, 0), 0)),
                  pl.BlockSpec((SUBLANES, pw), lambda i: (jnp.minimum((i + 1) * rb, last), 0)),
                  pl.BlockSpec(pool_w.shape, lambda i: (0, 0, 0)),
                  pl.BlockSpec((1, pw), lambda i: (0, 0))],
        out_specs=pl.BlockSpec((bm, pw), lambda i: (i, 0)),
        out_shape=jax.ShapeDtypeStruct((m, pw), BF16),
        compiler_params=_params("arbitrary"),
        name="multiscale_pool",
    )(u, u, u, pool_w, pool_scale.reshape(1, pw))


def _flash_tiles(q_tiles, k_chunk, vt_ref, finish, *, nk, bk):
    dv = vt_ref.shape[0]

    def update(t, s, state):
        m_run, l_run, acc = state
        m_new = jnp.maximum(m_run, jnp.max(s, axis=0, keepdims=True))
        alpha = jnp.exp2(m_run - m_new)
        p = jnp.exp2(s - m_new)
        l_new = alpha * l_run + jnp.sum(p, axis=0, keepdims=True)
        acc = alpha * acc + jnp.dot(vt_ref[:, t * bk:(t + 1) * bk], p.astype(BF16), preferred_element_type=F32)
        return m_new, l_new, acc

    pending = None
    for n, qs in enumerate(q_tiles):
        r = qs.shape[0]
        scores_of = lambda t, qs=qs: lax.dot_general(k_chunk(t), qs, (((1,), (1,)), ((), ())),
                                                     preferred_element_type=F32)
        state = (jnp.full((1, r), -jnp.inf, F32), jnp.zeros((1, r), F32), jnp.zeros((dv, r), F32))
        scores = scores_of(0)
        if pending is not None:
            pending()
        for t in range(nk - 1):
            nxt = scores_of(t + 1)
            state = update(t, scores, state)
            scores = nxt

        def pending(n=n, scores=scores, state=state):
            _, l_run, acc = update(nk - 1, scores, state)
            finish(n, l_run, acc)
    pending()


def _diff_attn_kernel(q_ref, k_ref, vt_ref, lq1_ref, lk1_ref, lq2_ref, lk2_ref, g_ref, *rest, bq, tiles, bk, nk,
                      lam_init):
    o_ref = rest[-1]
    bt = bq // tiles
    lane = lax.broadcasted_iota(jnp.int32, (bt, HEAD_DIM), 1)
    zero = jnp.zeros((bt, HEAD_DIM), q_ref.dtype)
    q_tiles = []
    for n in range(tiles):
        q = q_ref[n * bt:(n + 1) * bt, :]
        q_tiles.append(jnp.concatenate([jnp.where(lane < HEAD_DIM // 2, q, zero),
                                        jnp.where(lane >= HEAD_DIM // 2, q, zero)], axis=0))
    lam = (jnp.exp(jnp.sum(lq1_ref[...] * lk1_ref[...], axis=1, keepdims=True))
           - jnp.exp(jnp.sum(lq2_ref[...] * lk2_ref[...], axis=1, keepdims=True)) + lam_init)

    def finish(n, l_run, acc):
        o_t = acc / l_run
        o = (o_t[:, :bt] - lam * o_t[:, bt:]).T
        y = o * lax.rsqrt(jnp.mean(o * o, axis=-1, keepdims=True) + EPS)
        o_ref[n * bt:(n + 1) * bt, :] = (y * g_ref[...] * (1 - lam_init)).astype(o_ref.dtype)

    _flash_tiles(q_tiles, lambda t: k_ref[t * bk:(t + 1) * bk, :], vt_ref, finish, nk=nk, bk=bk)


def _out_alias(out, n_in):
    return jax.ShapeDtypeStruct(out.shape, out.dtype), [pl.BlockSpec(memory_space=pl.ANY)], [out], {n_in: 0}


def _diff_attn_call(qk, vt, lam_vecs, subln, out, *, heads, q_rows, q_row0, kv_rows, kv_row0, lam_init, name):
    bq = _pick(q_rows, (512, 256, 128))
    tiles = max(1, bq // 256)
    bk = _pick(kv_rows, (2816, 768, 512, 640, 384, 256, 128))
    qb0, kb0 = q_row0 // bq, kv_row0 // kv_rows
    vec = pl.BlockSpec((1, HEAD_DIM // 2), lambda h, i: (0, 0))
    out_shape, alias_specs, alias_args, aliases = _out_alias(out, 8)
    return pl.pallas_call(
        functools.partial(_diff_attn_kernel, bq=bq, tiles=tiles, bk=bk, nk=kv_rows // bk, lam_init=lam_init),
        grid=(heads, q_rows // bq),
        in_specs=[pl.BlockSpec((bq, HEAD_DIM), lambda h, i: (qb0 + i, h)),
                  pl.BlockSpec((kv_rows, HEAD_DIM), lambda h, i: (kb0, heads + h)),
                  pl.BlockSpec((HEAD_DIM, kv_rows), lambda h, i: (h, kb0)),
                  vec, vec, vec, vec,
                  pl.BlockSpec((1, HEAD_DIM), lambda h, i: (0, 0))] + alias_specs,
        out_specs=pl.BlockSpec((bq, HEAD_DIM), lambda h, i: (qb0 + i, h)),
        out_shape=out_shape,
        input_output_aliases=aliases,
        compiler_params=_params("arbitrary", "arbitrary"),
        name=name,
    )(qk, qk, vt, *lam_vecs, subln, *alias_args)


def _mla_attn_kernel(qn_ref, qp_ref, kn_ref, kp_ref, vt_ref, *rest, bq, tiles, bk, nk):
    o_ref = rest[-1]
    bt = bq // tiles
    q_tiles = [jnp.concatenate([qn_ref[n * bt:(n + 1) * bt, :], qp_ref[n * bt:(n + 1) * bt, :]], axis=1)
               for n in range(tiles)]

    def k_chunk(t):
        rows = slice(t * bk, (t + 1) * bk)
        return jnp.concatenate([kn_ref[rows, :], kp_ref[rows, :]], axis=1)

    def finish(n, l_run, acc):
        o_ref[n * bt:(n + 1) * bt, :] = (acc / l_run).T.astype(o_ref.dtype)

    _flash_tiles(q_tiles, k_chunk, vt_ref, finish, nk=nk, bk=bk)


def _mla_attn_call(qn, qp, kn, kpe, vt, out, *, heads, q_rows, q_row0, kv_rows, kv_row0, name):
    bq = _pick(q_rows, (1024, 512, 256, 128))
    tiles = max(1, bq // 512)
    bk = _pick(kv_rows, (2816, 768, 512, 640, 384, 256, 128))
    qb0, kb0 = q_row0 // bq, kv_row0 // kv_rows
    out_shape, alias_specs, alias_args, aliases = _out_alias(out, 5)
    return pl.pallas_call(
        functools.partial(_mla_attn_kernel, bq=bq, tiles=tiles, bk=bk, nk=kv_rows // bk),
        grid=(heads, q_rows // bq),
        in_specs=[pl.BlockSpec((bq, HEAD_DIM), lambda h, i: (qb0 + i, h)),
                  pl.BlockSpec((bq, HEAD_DIM), lambda h, i: (qb0 + i, h)),
                  pl.BlockSpec((kv_rows, HEAD_DIM), lambda h, i: (kb0, h)),
                  pl.BlockSpec((kv_rows, LANES), lambda h, i: (kb0, 0)),
                  pl.BlockSpec((HEAD_DIM, kv_rows), lambda h, i: (h, kb0))] + alias_specs,
        out_specs=pl.BlockSpec((bq, HEAD_DIM), lambda h, i: (qb0 + i, h)),
        out_shape=out_shape,
        input_output_aliases=aliases,
        compiler_params=_params("arbitrary", "arbitrary"),
        name=name,
    )(qn, qp, kn, kpe, vt, *alias_args)


def _rope_tables(seq, ctx_len, q_scale):
    rows = seq // GRID_W
    row = jnp.broadcast_to(jnp.arange(rows, dtype=F32)[:, None], (rows, GRID_W)).reshape(seq)
    col = jnp.broadcast_to(jnp.arange(GRID_W, dtype=F32)[None, :], (rows, GRID_W)).reshape(seq)
    n_freq = 16
    inv_freq = ROPE_BASE ** (-jnp.arange(n_freq, dtype=F32) / n_freq)
    ang = jnp.concatenate([row[:, None] * inv_freq, col[:, None] * inv_freq], axis=-1)
    cos, sin = jnp.cos(ang), jnp.sin(ang)
    zero = jnp.zeros_like(sin)
    c = jnp.concatenate([cos, cos, cos, cos], axis=-1)
    s1 = jnp.concatenate([-sin, zero, -sin, zero], axis=-1)
    s2 = jnp.concatenate([zero, sin, zero, sin], axis=-1)
    pad = lambda t, v: jnp.concatenate([t, jnp.full((ctx_len, LANES), v, F32)], axis=0)
    return tuple(jnp.concatenate([t, t * q_scale], axis=1) for t in (pad(c, 1.0), pad(s1, 0.0), pad(s2, 0.0)))


def kernel(x, c, ctx, c_ctx, ada_down, ada_up, ada_bias, norm_mix, norm_ffn, ffn_gate, ffn_up, ffn_down, ev_w_in, ev_conv, ev_lam_q1, ev_lam_k1, ev_lam_q2, ev_lam_k2, ev_subln, ev_w_out, od_w_in, od_q_norm, od_w_uq, od_kv_norm, od_w_ukv, od_pool_w, od_pool_scale, od_w_out, final_norm):
    _, seq, d = x.shape
    ctx_len = ctx.shape[1]
    m = seq + ctx_len
    depth = ada_down.shape[0]
    half = d // 2
    heads = half // HEAD_DIM
    q_rank = od_q_norm.shape[-1]
    kv_rank = od_kv_norm.shape[-1]
    rope_dim = HEAD_DIM // 2
    bf = lambda t: t.astype(BF16)

    r = jnp.concatenate([x[0], ctx[0]], axis=0)
    diff_q_scale = (HEAD_DIM // 2) ** -0.5 * LOG2E
    mla_q_scale = (HEAD_DIM + rope_dim) ** -0.5 * LOG2E
    tables_diff = _rope_tables(seq, ctx_len, diff_q_scale)
    tables_mla = _rope_tables(seq, ctx_len, mla_q_scale)

    cc = jnp.zeros((SUBLANES, d), F32).at[0].set(c[0]).at[1].set(c_ctx)
    mods = _ada_call(cc, ada_down, ada_up, ada_bias.reshape(depth, 1, -1))

    for l in range(depth):
        mod = mods[l]
        i = l // 2
        h = _norm_call(r, norm_mix[l], rows=m, out_dtype=BF16, mod=mod, shift_blk=0, scale_blk=1, n_lat=seq,
                       name="norm_mod_mix")
        if l % 2 == 0:
            lam_init = 0.8 - 0.6 * math.exp(-0.3 * l)
            big_rows = (1408, 1280, 768, 640, 512, 256, 128)
            pa = _wmm_call([h], [(ev_w_in, i, 0)], n=3 * half, out_dtype=F32, bm_cands=big_rows, name="ev_in_conv")
            qk = _wmm_call([h], [(ev_w_in, i, 0)], n=2 * half, col0=3 * half, out_dtype=BF16, epi="rope",
                           rope=tables_diff, rope_cols=(0, 2 * half), rope_q_cols=(0, half), bm_cands=big_rows,
                           name="ev_in_qk")
            vt = _wmm_call([h], [(ev_w_in, i, 0)], n=half, col0=5 * half, out_dtype=BF16, epi="transpose",
                           bm_cands=big_rows, name="ev_in_vt")
            ya = _conv_call(pa, ev_conv[i], n_lat=seq)
            lam_vecs = [v[i].reshape(1, -1) for v in (ev_lam_q1, ev_lam_k1, ev_lam_q2, ev_lam_k2)]
            subln = ev_subln[i].reshape(1, -1)
            yb = jnp.zeros((m, half), BF16)
            yb = _diff_attn_call(qk, vt, lam_vecs, subln, yb, heads=heads, q_rows=seq, q_row0=0, kv_rows=m,
                                 kv_row0=0, lam_init=lam_init, name="diff_attn")
            yb = _diff_attn_call(qk, vt, lam_vecs, subln, yb, heads=heads, q_rows=ctx_len, q_row0=seq,
                                 kv_rows=ctx_len, kv_row0=seq, lam_init=lam_init, name="diff_attn_ctx")
            y1, y2 = ya, yb
            w_out = ev_w_out
        else:
            w_in = od_w_in[i]
            w_kpe = jnp.pad(w_in[:, q_rank + kv_rank:q_rank + kv_rank + rope_dim], ((0, 0), (0, LANES - rope_dim)))
            cq = _mm_call(h, bf(w_in[:, :q_rank]), out_dtype=F32, name="od_in_cq")
            ckv = _mm_call(h, bf(w_in[:, q_rank:q_rank + kv_rank]), out_dtype=F32, name="od_in_ckv")
            kpe = _wmm_call([h], [(w_kpe, None, 0)], n=LANES, out_dtype=BF16, epi="rope", rope=tables_mla,
                            rope_cols=(0, LANES), name="od_in_kpe")
            u = _mm_call(h, bf(w_in[:, q_rank + kv_rank + rope_dim:]), out_dtype=F32, name="od_in_pool")
            cqn = _norm_call(cq, od_q_norm[i], rows=m, out_dtype=BF16, name="norm_cq")
            ckvn = _norm_call(ckv, od_kv_norm[i], rows=m, out_dtype=BF16, name="norm_ckv")
            w_uq = od_w_uq[i].reshape(q_rank, heads, HEAD_DIM + rope_dim)
            w_qn = w_uq[:, :, :HEAD_DIM].reshape(q_rank, half)
            w_qp = jnp.pad(w_uq[:, :, HEAD_DIM:], ((0, 0), (0, 0), (0, LANES - rope_dim))).reshape(q_rank, heads * LANES)
            wide = (2048, 1024, 512, 256, 128)
            qn = _wmm_call([cqn], [(w_qn, None, 0)], n=half, out_dtype=BF16, out_scale=mla_q_scale, bn_cands=wide,
                           name="od_uq_nope")
            qp = _wmm_call([cqn], [(w_qp, None, 0)], n=heads * LANES, out_dtype=BF16, epi="rope", rope=tables_mla,
                           rope_cols=(0, heads * LANES), rope_q_cols=(0, heads * LANES), bn_cands=wide,
                           name="od_uq_rope")
            w_ukv = od_w_ukv[i].reshape(kv_rank, heads, 2 * HEAD_DIM)
            kn = _wmm_call([ckvn], [(w_ukv[:, :, :HEAD_DIM].reshape(kv_rank, half), None, 0)], n=half, out_dtype=BF16,
                           bn_cands=wide, name="od_uk")
            vt = _wmm_call([ckvn], [(w_ukv[:, :, HEAD_DIM:].reshape(kv_rank, half), None, 0)], n=half, out_dtype=BF16,
                           epi="transpose", bn_cands=wide, name="od_uvt")
            ym = jnp.zeros((m, half), BF16)
            ym = _mla_attn_call(qn, qp, kn, kpe, vt, ym, heads=heads, q_rows=seq, q_row0=0, kv_rows=m, kv_row0=0,
                                name="mla_attn")
            ym = _mla_attn_call(qn, qp, kn, kpe, vt, ym, heads=heads, q_rows=ctx_len, q_row0=seq, kv_rows=ctx_len,
                                kv_row0=seq, name="mla_attn_ctx")
            y1 = _pool_call(u, bf(od_pool_w[i]), od_pool_scale[i], n_lat=seq)
            y2 = ym
            w_out = od_w_out
        r = _wmm_call([y1, y2], [(w_out, i, 0), (w_out, i, 1)], n=d, out_dtype=F32, epi="res", res=r, mod=mod,
                      gate_blk=2, n_lat=seq, name="mix_out")
        h2 = _norm_call(r, norm_ffn[l], rows=m, out_dtype=BF16, mod=mod, shift_blk=3, scale_blk=4, n_lat=seq,
                        name="norm_mod_ffn")
        a, w_down = _wmm_call([h2], [(ffn_gate, l, 0), (ffn_up, l, 0)], n=ffn_gate.shape[-1], out_dtype=BF16,
                              epi="swiglu", bm_cands=(1408, 1280, 768, 640, 512, 256, 128), bn_cands=(256, 128),
                              side_cast=(ffn_down, l), name="ffn_gate_up")
        r = _mm_res_call(a, w_down, r, mod, 5, n_lat=seq, name="ffn_down")
    out = _norm_call(r, final_norm, rows=seq, out_dtype=F32, name="final_norm")
    return out[None]
```

```python
import functools
import math

import jax
import jax.numpy as jnp
from jax import lax
from jax.experimental import pallas as pl
from jax.experimental.pallas import tpu as pltpu

F32 = jnp.float32
BF16 = jnp.bfloat16

LANES = 128
SUBLANES = 8
BF16_ROWS = 16
HEAD_DIM = 128
MXU_DIM = 256
GRID_W = 64
ROPE_BASE = 10000.0
EPS = 1e-6
POOL_WINDOWS = (2, 4, 8, 16)
POOL_HALO = 8
LOG2E = 1.4426950408889634
VMEM_LIMIT = 56 * 1024 * 1024


def _pick(n, cands):
    for c in cands:
        if c <= n and n % c == 0:
            return c
    return n


def _params(*sem):
    return pltpu.CompilerParams(dimension_semantics=sem, vmem_limit_bytes=VMEM_LIMIT)


def _is_ctx_rows(tile, bm, n_lat):
    rows = tile * bm + lax.broadcasted_iota(jnp.int32, (bm, 1), 0)
    return rows >= n_lat


def _sel(is_ctx, ref):
    return jnp.where(is_ctx, ref[1:2, :], ref[0:1, :])


def _ada_kernel(cc_ref, down_ref, up_ref, bias_ref, o_ref, t_ref):
    @pl.when(pl.program_id(1) == 0)
    def _():
        c = cc_ref[...]
        s = c * jax.nn.sigmoid(c)
        t = jnp.dot(s.astype(BF16), down_ref[...].astype(BF16), preferred_element_type=F32)
        t_ref[...] = t.astype(BF16)

    o_ref[...] = jnp.dot(t_ref[...], up_ref[...].astype(BF16), preferred_element_type=F32) + bias_ref[...]


def _ada_call(cc, down, up, bias):
    depth, d, rank = down.shape
    n = up.shape[-1]
    bn = _pick(n, (8192, 4096, 2048, 1024, 512, 256, 128))
    return pl.pallas_call(
        _ada_kernel,
        grid=(depth, n // bn),
        in_specs=[
            pl.BlockSpec((SUBLANES, d), lambda l, j: (0, 0)),
            pl.BlockSpec((None, d, rank), lambda l, j: (l, 0, 0)),
            pl.BlockSpec((None, rank, bn), lambda l, j: (l, 0, j)),
            pl.BlockSpec((None, 1, bn), lambda l, j: (l, 0, j)),
        ],
        out_specs=pl.BlockSpec((None, SUBLANES, bn), lambda l, j: (l, 0, j)),
        out_shape=jax.ShapeDtypeStruct((depth, SUBLANES, n), F32),
        scratch_shapes=[pltpu.VMEM((SUBLANES, rank), BF16)],
        compiler_params=_params("arbitrary", "arbitrary"),
        name="ada_mod",
    )(cc, down, up, bias)


NORM_ROWS = 16


def _norm_kernel(*refs, modulate, bm, n_lat):
    if modulate:
        x_ref, g_ref, sh_ref, sc_ref, o_ref = refs
    else:
        x_ref, g_ref, o_ref = refs
    g = g_ref[...]
    row0 = pl.program_id(0) * bm

    def step(r, _):
        rows = pl.ds(pl.multiple_of(r * NORM_ROWS, NORM_ROWS), NORM_ROWS)
        x = x_ref[rows, :].astype(F32)
        y = x * lax.rsqrt(jnp.mean(x * x, axis=-1, keepdims=True) + EPS)
        y = y * g
        if modulate:
            is_ctx = row0 + r * NORM_ROWS >= n_lat
            y = y * (1 + _sel(is_ctx, sc_ref)) + _sel(is_ctx, sh_ref)
        o_ref[rows, :] = y.astype(o_ref.dtype)
        return 0

    lax.fori_loop(0, bm // NORM_ROWS, step, 0, unroll=4)


def _norm_call(x, g, *, rows, out_dtype, mod=None, shift_blk=0, scale_blk=0, n_lat=0, name="rmsnorm"):
    k = x.shape[-1]
    bm = _pick(rows, (384, 320, 256, 128, 64, 32, 16, 8))
    in_specs = [pl.BlockSpec((bm, k), lambda i: (i, 0)), pl.BlockSpec((1, k), lambda i: (0, 0))]
    args = [x, g.reshape(1, k)]
    if mod is not None:
        in_specs += [pl.BlockSpec((SUBLANES, k), lambda i: (0, shift_blk)),
                     pl.BlockSpec((SUBLANES, k), lambda i: (0, scale_blk))]
        args += [mod, mod]
    return pl.pallas_call(
        functools.partial(_norm_kernel, modulate=mod is not None, bm=bm, n_lat=n_lat),
        grid=(rows // bm,),
        in_specs=in_specs,
        out_specs=pl.BlockSpec((bm, k), lambda i: (i, 0)),
        out_shape=jax.ShapeDtypeStruct((rows, k), out_dtype),
        compiler_params=_params("arbitrary"),
        name=name,
    )(*args)


def _rope(x, c, s1, s2):
    out = []
    for b in range(x.shape[-1] // LANES):
        xb = x[:, b * LANES:(b + 1) * LANES]
        up = pltpu.roll(xb, LANES - 32, 1)
        dn = pltpu.roll(xb, 32, 1)
        out.append(xb * c + up * s1 + dn * s2)
    return jnp.concatenate(out, axis=1)


def _rope_table_specs(bm, bn, rope_q_cols, row_col):
    q0, q1 = (0, 0) if rope_q_cols is None else (rope_q_cols[0] // bn, rope_q_cols[1] // bn)

    def index(*grid):
        i, j = row_col(*grid)
        return i, jnp.where((j >= q0) & (j < q1), 1, 0)

    return [pl.BlockSpec((bm, LANES), index)] * 3


def _mm_kernel(a_ref, w_ref, o_ref):
    o_ref[...] = jnp.dot(a_ref[...], w_ref[...], preferred_element_type=F32).astype(o_ref.dtype)


def _mm_call(a, w, *, out_dtype, name):
    m, k = a.shape
    n = w.shape[-1]
    bm = _pick(m, (768, 640, 512, 384, 256, 128))
    bn = _pick(n, (512, 896, 256, 128))
    return pl.pallas_call(
        _mm_kernel,
        grid=(m // bm, n // bn),
        in_specs=[pl.BlockSpec((bm, k), lambda i, j: (i, 0)), pl.BlockSpec((k, bn), lambda i, j: (0, j))],
        out_specs=pl.BlockSpec((bm, bn), lambda i, j: (i, j)),
        out_shape=jax.ShapeDtypeStruct((m, n), out_dtype),
        compiler_params=_params("arbitrary", "arbitrary"),
        name=name,
    )(a, w)


def _mm_res_kernel(a_ref, w_ref, res_ref, gate_ref, o_ref, *, bm, n_lat):
    acc = jnp.dot(a_ref[...], w_ref[...], preferred_element_type=F32)
    gate = _sel(_is_ctx_rows(pl.program_id(0), bm, n_lat), gate_ref)
    o_ref[...] = res_ref[...] + gate * acc


def _mm_res_call(a, w, res, mod, gate_blk, *, n_lat, name):
    m, n = res.shape
    k = a.shape[-1]
    bm = _pick(m, (768, 640, 384, 256, 128))
    bn = _pick(n, (256, 128))
    return pl.pallas_call(
        functools.partial(_mm_res_kernel, bm=bm, n_lat=n_lat),
        grid=(m // bm, n // bn),
        in_specs=[pl.BlockSpec((bm, k), lambda i, j: (i, 0)),
                  pl.BlockSpec((k, bn), lambda i, j: (0, j)),
                  pl.BlockSpec((bm, bn), lambda i, j: (i, j)),
                  pl.BlockSpec((SUBLANES, bn), lambda i, j: (0, gate_blk * (n // bn) + j))],
        out_specs=pl.BlockSpec((bm, bn), lambda i, j: (i, j)),
        out_shape=jax.ShapeDtypeStruct((m, n), F32),
        compiler_params=_params("arbitrary", "arbitrary"),
        name=name,
    )(a, w, res, mod)


CAST_ROWS = 256


def _cast_weights(w_refs, wb_refs):
    for w_ref, wb_ref in zip(w_refs, wb_refs):
        nrows = math.gcd(w_ref.shape[0], CAST_ROWS)

        def step(r, _, w_ref=w_ref, wb_ref=wb_ref, nrows=nrows):
            rows = pl.ds(pl.multiple_of(r * nrows, nrows), nrows)
            wb_ref[rows, :] = w_ref[rows, :].astype(BF16)
            return 0
        lax.fori_loop(0, w_ref.shape[0] // nrows, step, 0)


ROPE_ROW_CHUNKS = 4


def _wmm_kernel(*refs, parts, epi, bm, n_lat, rope_tiles, rope_all, side, out_scale):
    n_extra = {"plain": 0, "transpose": 0, "rope": 3, "res": 2, "swiglu": 0}[epi]
    n_w = 2 * parts if epi == "swiglu" else parts
    a_refs, w_refs = refs[:parts], refs[parts:parts + n_w]
    extra = refs[parts + n_w:parts + n_w + n_extra]
    rest = refs[parts + n_w + n_extra:]
    if side:
        side_in, o_ref, side_out = rest[:3]
        wb_refs = rest[3:]
        side_out[...] = side_in[...].astype(BF16)
    else:
        o_ref, wb_refs = rest[0], rest[1:]
    j, i = pl.program_id(0), pl.program_id(1)

    @pl.when(i == 0)
    def _():
        _cast_weights(w_refs, wb_refs)

    def mm(wbs, rows=slice(None)):
        acc = jnp.dot(a_refs[0][rows, :], wbs[0][...], preferred_element_type=F32)
        for a_ref, wb_ref in zip(a_refs[1:], wbs[1:]):
            acc = acc + jnp.dot(a_ref[rows, :], wb_ref[...], preferred_element_type=F32)
        return acc

    if epi == "swiglu":
        g = mm(wb_refs[:parts])
        u = mm(wb_refs[parts:])
        o_ref[...] = (g * jax.nn.sigmoid(g) * u).astype(o_ref.dtype)
    elif epi == "res":
        res_ref, gate_ref = extra
        gate = _sel(_is_ctx_rows(i, bm, n_lat), gate_ref)
        o_ref[...] = res_ref[...] + gate * mm(wb_refs)
    elif epi == "rope" and rope_all:
        rc = bm // ROPE_ROW_CHUNKS
        pending = None
        for c in range(ROPE_ROW_CHUNKS):
            rows = slice(c * rc, (c + 1) * rc)
            acc = mm(wb_refs, rows)
            if pending is not None:
                pending()

            def pending(rows=rows, acc=acc):
                o_ref[rows, :] = _rope(acc, *(t[rows, :] for t in extra)).astype(o_ref.dtype)
        pending()
    elif epi == "rope":
        acc = mm(wb_refs)
        roped = (j >= rope_tiles[0]) & (j < rope_tiles[1])

        @pl.when(roped)
        def _():
            o_ref[...] = _rope(acc, *(t[...] for t in extra)).astype(o_ref.dtype)

        @pl.when(jnp.logical_not(roped))
        def _():
            o_ref[...] = acc.astype(o_ref.dtype)
    elif epi == "transpose":
        o_ref[...] = mm(wb_refs).T.astype(o_ref.dtype)
    else:
        o_ref[...] = (mm(wb_refs) * out_scale).astype(o_ref.dtype)


def _wmm_call(a_list, weights, *, n, col0=0, out_dtype, epi="plain", rope=None, rope_cols=None, rope_q_cols=None,
              res=None, mod=None,
              gate_blk=0, n_lat=0, bm_cands=(768, 640, 512, 384, 256, 128), bn_cands=(512, 256, 128),
              side_cast=None, out_scale=1.0, name="wmm"):
    m = a_list[0].shape[0]
    parts = len(a_list)
    bm = _pick(m, bm_cands)
    bn = _pick(math.gcd(n, col0) if col0 else n, bn_cands)
    cb0 = col0 // bn
    in_specs = [pl.BlockSpec((bm, a.shape[1]), lambda j, i: (i, 0)) for a in a_list]
    args = list(a_list)
    scratch = []
    for idx, (w, layer, rb) in enumerate(weights):
        k = a_list[idx % parts].shape[1]
        if layer is None:
            in_specs.append(pl.BlockSpec((k, bn), lambda j, i, rb=rb: (rb, cb0 + j)))
        else:
            in_specs.append(pl.BlockSpec((None, k, bn), lambda j, i, layer=layer, rb=rb: (layer, rb, cb0 + j)))
        args.append(w)
        scratch.append(pltpu.VMEM((k, bn), BF16))
    rope_tiles, rope_all = None, False
    if epi == "rope":
        rope_tiles = (rope_cols[0] // bn, rope_cols[1] // bn)
        rope_all = rope_tiles == (0, n // bn) and bm % (ROPE_ROW_CHUNKS * BF16_ROWS) == 0
        in_specs += _rope_table_specs(bm, bn, rope_q_cols, lambda j, i: (i, j))
        args += list(rope)
    elif epi == "res":
        in_specs += [pl.BlockSpec((bm, bn), lambda j, i: (i, j)),
                     pl.BlockSpec((SUBLANES, bn), lambda j, i: (0, gate_blk * (n // bn) + j))]
        args += [res, mod]
    out_specs = (pl.BlockSpec((bn, bm), lambda j, i: (j, i)) if epi == "transpose"
                 else pl.BlockSpec((bm, bn), lambda j, i: (i, j)))
    out_shape = jax.ShapeDtypeStruct((n, m) if epi == "transpose" else (m, n), out_dtype)
    if side_cast is not None:
        w_side, side_layer = side_cast
        _, side_rows, side_cols = w_side.shape
        steps, ni = (n // bn) * (m // bm), m // bm
        slab = next(r for r in (64, 128, 256, 512, 1024, side_rows) if side_rows % r == 0 and side_rows // r <= steps)
        last = side_rows // slab - 1
        in_specs.append(pl.BlockSpec((None, slab, side_cols),
                                     lambda j, i: (side_layer, jnp.minimum(j * ni + i, last), 0)))
        args.append(w_side)
        out_specs = (out_specs, pl.BlockSpec((slab, side_cols), lambda j, i: (jnp.minimum(j * ni + i, last), 0)))
        out_shape = (out_shape, jax.ShapeDtypeStruct((side_rows, side_cols), BF16))
    return pl.pallas_call(
        functools.partial(_wmm_kernel, parts=parts, epi=epi, bm=bm, n_lat=n_lat, rope_tiles=rope_tiles,
                          rope_all=rope_all, side=side_cast is not None, out_scale=out_scale),
        grid=(n // bn, m // bm),
        in_specs=in_specs,
        out_specs=out_specs,
        out_shape=out_shape,
        scratch_shapes=scratch,
        compiler_params=_params("arbitrary", "arbitrary"),
        name=name,
    )(*args)


def _conv_kernel(bg_ref, cg_ref, xa_ref, cgp_ref, xap_ref, cgn_ref, xan_ref, w_ref, o_ref, *, bm, n_lat, m):
    i = pl.program_id(0)
    u = cg_ref[...] * xa_ref[...]
    u_prev = cgp_ref[SUBLANES - 1:SUBLANES, :] * xap_ref[SUBLANES - 1:SUBLANES, :]
    u_next = cgn_ref[0:1, :] * xan_ref[0:1, :]
    loc = lax.broadcasted_iota(jnp.int32, (bm, 1), 0)
    rows = i * bm + loc
    prev = jnp.where(loc == 0, u_prev, pltpu.roll(u, 1, 0))
    prev = jnp.where((rows == 0) | (rows == n_lat), 0.0, prev)
    nxt = jnp.where(loc == bm - 1, u_next, pltpu.roll(u, bm - 1, 0))
    nxt = jnp.where((rows == n_lat - 1) | (rows == m - 1), 0.0, nxt)
    y = w_ref[0:1, :] * prev + w_ref[1:2, :] * u + w_ref[2:3, :] * nxt
    o_ref[...] = (bg_ref[...] * y).astype(o_ref.dtype)


def _conv_call(pa, conv_w, *, n_lat):
    m = pa.shape[0]
    aw = conv_w.shape[-1]
    bm = _pick(math.gcd(n_lat, m - n_lat), (256, 128, 64, 32, 16, 8))
    bc = _pick(aw, (2048, 1024, 512, 256, 128))
    nc = aw // bc
    rb = bm // SUBLANES
    last = m // SUBLANES - 1
    main = lambda off: pl.BlockSpec((bm, bc), lambda i, c: (i, off * nc + c))
    prev = lambda off: pl.BlockSpec((SUBLANES, bc), lambda i, c: (jnp.maximum(i * rb - 1, 0), off * nc + c))
    nxt = lambda off: pl.BlockSpec((SUBLANES, bc), lambda i, c: (jnp.minimum((i + 1) * rb, last), off * nc + c))
    return pl.pallas_call(
        functools.partial(_conv_kernel, bm=bm, n_lat=n_lat, m=m),
        grid=(m // bm, nc),
        in_specs=[main(0), main(1), main(2), prev(1), prev(2), nxt(1), nxt(2),
                  pl.BlockSpec((conv_w.shape[0], bc), lambda i, c: (0, c))],
        out_specs=pl.BlockSpec((bm, bc), lambda i, c: (i, c)),
        out_shape=jax.ShapeDtypeStruct((m, aw), BF16),
        compiler_params=_params("arbitrary", "arbitrary"),
        name="short_conv",
    )(pa, pa, pa, pa, pa, pa, pa, conv_w)


def _pool_kernel(u_ref, up_ref, un_ref, w_ref, sc_ref, o_ref, *, bm, n_lat, m, pg):
    i = pl.program_id(0)
    ext = bm + 2 * POOL_HALO
    is_ctx = i * bm >= n_lat
    lo = jnp.where(is_ctx, n_lat, 0)
    hi = jnp.where(is_ctx, m, n_lat)
    rows_ext = i * bm - POOL_HALO + lax.broadcasted_iota(jnp.int32, (ext, 1), 0)
    valid = (rows_ext >= lo) & (rows_ext < hi)
    t = i * bm + lax.broadcasted_iota(jnp.int32, (bm, 1), 0) - lo
    t_len = hi - lo
    for g, win in enumerate(POOL_WINDOWS):
        cols = slice(g * pg, (g + 1) * pg)
        u = u_ref[:, cols]
        e = jnp.concatenate([up_ref[:, cols], u, un_ref[:, cols]], axis=0)
        e = jnp.where(valid, e, 0.0)
        span = 1
        while span < win:
            e = e + pltpu.roll(e, ext - span, 0)
            span *= 2
        start = POOL_HALO - win // 2
        if start:
            e = pltpu.roll(e, ext - start, 0)
        wsum = e[0:bm, :]
        w_lo = jnp.clip(t - win // 2, 0, t_len - 1)
        w_hi = jnp.clip(t + win // 2 - 1, 0, t_len - 1)
        cnt = (w_hi - w_lo + 1).astype(F32)
        diff = (wsum / cnt - u).astype(BF16)
        y = jnp.dot(diff, w_ref[g], preferred_element_type=F32) * sc_ref[:, cols]
        o_ref[:, cols] = y.astype(o_ref.dtype)


def _pool_call(u, pool_w, pool_scale, *, n_lat):
    m, pw = u.shape
    pg = pw // len(POOL_WINDOWS)
    bm = _pick(math.gcd(n_lat, m - n_lat), (256, 128, 64, 32, 16, 8))
    rb = bm // SUBLANES
    last = m // SUBLANES - 1
    return pl.pallas_call(
        functools.partial(_pool_kernel, bm=bm, n_lat=n_lat, m=m, pg=pg),
        grid=(m // bm,),
        in_specs=[pl.BlockSpec((bm, pw), lambda i: (i, 0)),
                  pl.BlockSpec((SUBLANES, pw), lambda i: (jnp.maximum(i * rb - 1, 0), 0)),
                  pl.BlockSpec((SUBLANES, pw), lambda i: (jnp.minimum((i + 1) * rb, last), 0)),
                  pl.BlockSpec(pool_w.shape, lambda i: (0, 0, 0)),
                  pl.BlockSpec((1, pw), lambda i: (0, 0))],
        out_specs=pl.BlockSpec((bm, pw), lambda i: (i, 0)),
        out_shape=jax.ShapeDtypeStruct((m, pw), BF16),
        compiler_params=_params("arbitrary"),
        name="multiscale_pool",
    )(u, u, u, pool_w, pool_scale.reshape(1, pw))


def _flash_tiles(q_tiles, k_rows, vt_ref, finish, *, chunks):
    dv = vt_ref.shape[0]

    nk = len(chunks)

    def update(t, s, state):
        m_run, l_run, acc = state
        m_new = jnp.maximum(m_run, jnp.max(s, axis=0, keepdims=True))
        alpha = jnp.exp2(m_run - m_new)
        p = jnp.exp2(s - m_new)
        l_new = alpha * l_run + jnp.sum(p, axis=0, keepdims=True)
        acc = alpha * acc + jnp.dot(vt_ref[:, chunks[t]], p.astype(BF16), preferred_element_type=F32)
        return m_new, l_new, acc

    pending = None
    for n, qs in enumerate(q_tiles):
        r = qs.shape[0]
        scores_of = lambda t, qs=qs: lax.dot_general(k_rows(chunks[t]), qs, (((1,), (1,)), ((), ())),
                                                     preferred_element_type=F32)
        state = (jnp.full((1, r), -jnp.inf, F32), jnp.zeros((1, r), F32), jnp.zeros((dv, r), F32))
        scores = scores_of(0)
        if pending is not None:
            pending()
        for t in range(nk - 1):
            nxt = scores_of(t + 1)
            state = update(t, scores, state)
            scores = nxt

        def pending(n=n, scores=scores, state=state):
            _, l_run, acc = update(nk - 1, scores, state)
            finish(n, l_run, acc)
    pending()


def _diff_attn_kernel(q_ref, k_ref, vt_ref, lq1_ref, lk1_ref, lq2_ref, lk2_ref, g_ref, *rest, bq, tiles, chunks,
                      lam_init):
    o_ref = rest[-1]
    bt = bq // tiles
    lane = lax.broadcasted_iota(jnp.int32, (bt, HEAD_DIM), 1)
    zero = jnp.zeros((bt, HEAD_DIM), q_ref.dtype)
    q_tiles = []
    for n in range(tiles):
        q = q_ref[n * bt:(n + 1) * bt, :]
        q_tiles.append(jnp.concatenate([jnp.where(lane < HEAD_DIM // 2, q, zero),
                                        jnp.where(lane >= HEAD_DIM // 2, q, zero)], axis=0))
    lam = (jnp.exp(jnp.sum(lq1_ref[...] * lk1_ref[...], axis=1, keepdims=True))
           - jnp.exp(jnp.sum(lq2_ref[...] * lk2_ref[...], axis=1, keepdims=True)) + lam_init)

    def finish(n, l_run, acc):
        o_t = acc / l_run
        o = (o_t[:, :bt] - lam * o_t[:, bt:]).T
        y = o * lax.rsqrt(jnp.mean(o * o, axis=-1, keepdims=True) + EPS)
        o_ref[n * bt:(n + 1) * bt, :] = (y * g_ref[...] * (1 - lam_init)).astype(o_ref.dtype)

    _flash_tiles(q_tiles, lambda rows: k_ref[rows, :], vt_ref, finish, chunks=chunks)


DIFF_CHUNK_SHARES = (2, 5, 6, 6, 6, 5, 3)
MLA_CHUNK_SHARES = (5, 10, 10, 8)


def _kv_chunks(kv_rows, shares):
    unit, total = MXU_DIM, sum(shares)
    if kv_rows % (unit * total):
        return (slice(0, kv_rows),)
    sizes = [kv_rows // total * share for share in shares]
    starts = [sum(sizes[:t]) for t in range(len(sizes))]
    return tuple(slice(a, a + n) for a, n in zip(starts, sizes))


def _out_alias(out, n_in):
    return jax.ShapeDtypeStruct(out.shape, out.dtype), [pl.BlockSpec(memory_space=pl.ANY)], [out], {n_in: 0}


def _diff_attn_call(qk, vt, lam_vecs, subln, out, *, heads, q_rows, q_row0, kv_rows, kv_row0, lam_init, name):
    bq = _pick(q_rows, (512, 256, 128))
    tiles = max(1, bq // 256)
    qb0, kb0 = q_row0 // bq, kv_row0 // kv_rows
    vec = pl.BlockSpec((1, HEAD_DIM // 2), lambda h, i: (0, 0))
    out_shape, alias_specs, alias_args, aliases = _out_alias(out, 8)
    return pl.pallas_call(
        functools.partial(_diff_attn_kernel, bq=bq, tiles=tiles, chunks=_kv_chunks(kv_rows, DIFF_CHUNK_SHARES), lam_init=lam_init),
        grid=(heads, q_rows // bq),
        in_specs=[pl.BlockSpec((bq, HEAD_DIM), lambda h, i: (qb0 + i, h)),
                  pl.BlockSpec((kv_rows, HEAD_DIM), lambda h, i: (kb0, heads + h)),
                  pl.BlockSpec((HEAD_DIM, kv_rows), lambda h, i: (h, kb0)),
                  vec, vec, vec, vec,
                  pl.BlockSpec((1, HEAD_DIM), lambda h, i: (0, 0))] + alias_specs,
        out_specs=pl.BlockSpec((bq, HEAD_DIM), lambda h, i: (qb0 + i, h)),
        out_shape=out_shape,
        input_output_aliases=aliases,
        compiler_params=_params("arbitrary", "arbitrary"),
        name=name,
    )(qk, qk, vt, *lam_vecs, subln, *alias_args)


def _mla_attn_kernel(qn_ref, qp_ref, kn_ref, kp_ref, vt_ref, *rest, bq, tiles, chunks):
    o_ref = rest[-1]
    bt = bq // tiles
    q_tiles = [jnp.concatenate([qn_ref[n * bt:(n + 1) * bt, :], qp_ref[n * bt:(n + 1) * bt, :]], axis=1)
               for n in range(tiles)]

    def k_rows(rows):
        return jnp.concatenate([kn_ref[rows, :], kp_ref[rows, :]], axis=1)

    def finish(n, l_run, acc):
        o_ref[n * bt:(n + 1) * bt, :] = (acc / l_run).T.astype(o_ref.dtype)

    _flash_tiles(q_tiles, k_rows, vt_ref, finish, chunks=chunks)


def _mla_attn_call(qn, qp, kn, kpe, vt, out, *, heads, q_rows, q_row0, kv_rows, kv_row0, name):
    bq = _pick(q_rows, (1024, 512, 256, 128))
    tiles = max(1, bq // 512)
    qb0, kb0 = q_row0 // bq, kv_row0 // kv_rows
    out_shape, alias_specs, alias_args, aliases = _out_alias(out, 5)
    return pl.pallas_call(
        functools.partial(_mla_attn_kernel, bq=bq, tiles=tiles, chunks=_kv_chunks(kv_rows, MLA_CHUNK_SHARES)),
        grid=(heads, q_rows // bq),
        in_specs=[pl.BlockSpec((bq, HEAD_DIM), lambda h, i: (qb0 + i, h)),
                  pl.BlockSpec((bq, HEAD_DIM), lambda h, i: (qb0 + i, h)),
                  pl.BlockSpec((kv_rows, HEAD_DIM), lambda h, i: (kb0, h)),
                  pl.BlockSpec((kv_rows, LANES), lambda h, i: (kb0, 0)),
                  pl.BlockSpec((HEAD_DIM, kv_rows), lambda h, i: (h, kb0))] + alias_specs,
        out_specs=pl.BlockSpec((bq, HEAD_DIM), lambda h, i: (qb0 + i, h)),
        out_shape=out_shape,
        input_output_aliases=aliases,
        compiler_params=_params("arbitrary", "arbitrary"),
        name=name,
    )(qn, qp, kn, kpe, vt, *alias_args)


def _rope_tables(seq, ctx_len, q_scale):
    rows = seq // GRID_W
    row = jnp.broadcast_to(jnp.arange(rows, dtype=F32)[:, None], (rows, GRID_W)).reshape(seq)
    col = jnp.broadcast_to(jnp.arange(GRID_W, dtype=F32)[None, :], (rows, GRID_W)).reshape(seq)
    n_freq = 16
    inv_freq = ROPE_BASE ** (-jnp.arange(n_freq, dtype=F32) / n_freq)
    ang = jnp.concatenate([row[:, None] * inv_freq, col[:, None] * inv_freq], axis=-1)
    cos, sin = jnp.cos(ang), jnp.sin(ang)
    zero = jnp.zeros_like(sin)
    c = jnp.concatenate([cos, cos, cos, cos], axis=-1)
    s1 = jnp.concatenate([-sin, zero, -sin, zero], axis=-1)
    s2 = jnp.concatenate([zero, sin, zero, sin], axis=-1)
    pad = lambda t, v: jnp.concatenate([t, jnp.full((ctx_len, LANES), v, F32)], axis=0)
    return tuple(jnp.concatenate([t, t * q_scale], axis=1) for t in (pad(c, 1.0), pad(s1, 0.0), pad(s2, 0.0)))


def kernel(x, c, ctx, c_ctx, ada_down, ada_up, ada_bias, norm_mix, norm_ffn, ffn_gate, ffn_up, ffn_down, ev_w_in, ev_conv, ev_lam_q1, ev_lam_k1, ev_lam_q2, ev_lam_k2, ev_subln, ev_w_out, od_w_in, od_q_norm, od_w_uq, od_kv_norm, od_w_ukv, od_pool_w, od_pool_scale, od_w_out, final_norm):
    _, seq, d = x.shape
    ctx_len = ctx.shape[1]
    m = seq + ctx_len
    depth = ada_down.shape[0]
    half = d // 2
    heads = half // HEAD_DIM
    q_rank = od_q_norm.shape[-1]
    kv_rank = od_kv_norm.shape[-1]
    rope_dim = HEAD_DIM // 2
    bf = lambda t: t.astype(BF16)

    r = jnp.concatenate([x[0], ctx[0]], axis=0)
    diff_q_scale = (HEAD_DIM // 2) ** -0.5 * LOG2E
    mla_q_scale = (HEAD_DIM + rope_dim) ** -0.5 * LOG2E
    tables_diff = _rope_tables(seq, ctx_len, diff_q_scale)
    tables_mla = _rope_tables(seq, ctx_len, mla_q_scale)

    cc = jnp.zeros((SUBLANES, d), F32).at[0].set(c[0]).at[1].set(c_ctx)
    mods = _ada_call(cc, ada_down, ada_up, ada_bias.reshape(depth, 1, -1))

    for l in range(depth):
        mod = mods[l]
        i = l // 2
        h = _norm_call(r, norm_mix[l], rows=m, out_dtype=BF16, mod=mod, shift_blk=0, scale_blk=1, n_lat=seq,
                       name="norm_mod_mix")
        if l % 2 == 0:
            lam_init = 0.8 - 0.6 * math.exp(-0.3 * l)
            big_rows = (1408, 1280, 768, 640, 512, 256, 128)
            pa = _wmm_call([h], [(ev_w_in, i, 0)], n=3 * half, out_dtype=F32, bm_cands=big_rows, name="ev_in_conv")
            qk = _wmm_call([h], [(ev_w_in, i, 0)], n=2 * half, col0=3 * half, out_dtype=BF16, epi="rope",
                           rope=tables_diff, rope_cols=(0, 2 * half), rope_q_cols=(0, half), bm_cands=big_rows,
                           name="ev_in_qk")
            vt = _wmm_call([h], [(ev_w_in, i, 0)], n=half, col0=5 * half, out_dtype=BF16, epi="transpose",
                           bm_cands=big_rows, name="ev_in_vt")
            ya = _conv_call(pa, ev_conv[i], n_lat=seq)
            lam_vecs = [v[i].reshape(1, -1) for v in (ev_lam_q1, ev_lam_k1, ev_lam_q2, ev_lam_k2)]
            subln = ev_subln[i].reshape(1, -1)
            yb = jnp.zeros((m, half), BF16)
            yb = _diff_attn_call(qk, vt, lam_vecs, subln, yb, heads=heads, q_rows=seq, q_row0=0, kv_rows=m,
                                 kv_row0=0, lam_init=lam_init, name="diff_attn")
            yb = _diff_attn_call(qk, vt, lam_vecs, subln, yb, heads=heads, q_rows=ctx_len, q_row0=seq,
                                 kv_rows=ctx_len, kv_row0=seq, lam_init=lam_init, name="diff_attn_ctx")
            y1, y2 = ya, yb
            w_out = ev_w_out
        else:
            w_in = od_w_in[i]
            w_kpe = jnp.pad(w_in[:, q_rank + kv_rank:q_rank + kv_rank + rope_dim], ((0, 0), (0, LANES - rope_dim)))
            cq = _mm_call(h, bf(w_in[:, :q_rank]), out_dtype=F32, name="od_in_cq")
            ckv = _mm_call(h, bf(w_in[:, q_rank:q_rank + kv_rank]), out_dtype=F32, name="od_in_ckv")
            kpe = _wmm_call([h], [(w_kpe, None, 0)], n=LANES, out_dtype=BF16, epi="rope", rope=tables_mla,
                            rope_cols=(0, LANES), name="od_in_kpe")
            u = _mm_call(h, bf(w_in[:, q_rank + kv_rank + rope_dim:]), out_dtype=F32, name="od_in_pool")
            cqn = _norm_call(cq, od_q_norm[i], rows=m, out_dtype=BF16, name="norm_cq")
            ckvn = _norm_call(ckv, od_kv_norm[i], rows=m, out_dtype=BF16, name="norm_ckv")
            w_uq = od_w_uq[i].reshape(q_rank, heads, HEAD_DIM + rope_dim)
            w_qn = w_uq[:, :, :HEAD_DIM].reshape(q_rank, half)
            w_qp = jnp.pad(w_uq[:, :, HEAD_DIM:], ((0, 0), (0, 0), (0, LANES - rope_dim))).reshape(q_rank, heads * LANES)
            wide = (2048, 1024, 512, 256, 128)
            qn = _wmm_call([cqn], [(w_qn, None, 0)], n=half, out_dtype=BF16, out_scale=mla_q_scale, bn_cands=wide,
                           name="od_uq_nope")
            qp = _wmm_call([cqn], [(w_qp, None, 0)], n=heads * LANES, out_dtype=BF16, epi="rope", rope=tables_mla,
                           rope_cols=(0, heads * LANES), rope_q_cols=(0, heads * LANES), bn_cands=wide,
                           name="od_uq_rope")
            w_ukv = od_w_ukv[i].reshape(kv_rank, heads, 2 * HEAD_DIM)
            kn = _wmm_call([ckvn], [(w_ukv[:, :, :HEAD_DIM].reshape(kv_rank, half), None, 0)], n=half, out_dtype=BF16,
                           bn_cands=wide, name="od_uk")
            vt = _wmm_call([ckvn], [(w_ukv[:, :, HEAD_DIM:].reshape(kv_rank, half), None, 0)], n=half, out_dtype=BF16,
                           epi="transpose", bn_cands=wide, name="od_uvt")
            ym = jnp.zeros((m, half), BF16)
            ym = _mla_attn_call(qn, qp, kn, kpe, vt, ym, heads=heads, q_rows=seq, q_row0=0, kv_rows=m, kv_row0=0,
                                name="mla_attn")
            ym = _mla_attn_call(qn, qp, kn, kpe, vt, ym, heads=heads, q_rows=ctx_len, q_row0=seq, kv_rows=ctx_len,
                                kv_row0=seq, name="mla_attn_ctx")
            y1 = _pool_call(u, bf(od_pool_w[i]), od_pool_scale[i], n_lat=seq)
            y2 = ym
            w_out = od_w_out
        r = _wmm_call([y1, y2], [(w_out, i, 0), (w_out, i, 1)], n=d, out_dtype=F32, epi="res", res=r, mod=mod,
                      gate_blk=2, n_lat=seq, name="mix_out")
        h2 = _norm_call(r, norm_ffn[l], rows=m, out_dtype=BF16, mod=mod, shift_blk=3, scale_blk=4, n_lat=seq,
                        name="norm_mod_ffn")
        a, w_down = _wmm_call([h2], [(ffn_gate, l, 0), (ffn_up, l, 0)], n=ffn_gate.shape[-1], out_dtype=BF16,
                              epi="swiglu", bm_cands=(1408, 1280, 768, 640, 512, 256, 128), bn_cands=(256, 128),
                              side_cast=(ffn_down, l), name="ffn_gate_up")
        r = _mm_res_call(a, w_down, r, mod, 5, n_lat=seq, name="ffn_down")
    out = _norm_call(r, final_norm, rows=seq, out_dtype=F32, name="final_norm")
    return out[None]
```

```python
import functools
import math

import jax
import jax.numpy as jnp
from jax import lax
from jax.experimental import pallas as pl
from jax.experimental.pallas import tpu as pltpu

F32 = jnp.float32
BF16 = jnp.bfloat16

LANES = 128
SUBLANES = 8
BF16_ROWS = 16
HEAD_DIM = 128
MXU_DIM = 256
GRID_W = 64
ROPE_BASE = 10000.0
EPS = 1e-6
POOL_WINDOWS = (2, 4, 8, 16)
POOL_HALO = 8
LOG2E = 1.4426950408889634
VMEM_LIMIT = 56 * 1024 * 1024


def _pick(n, cands):
    for c in cands:
        if c <= n and n % c == 0:
            return c
    return n


def _params(*sem):
    return pltpu.CompilerParams(dimension_semantics=sem, vmem_limit_bytes=VMEM_LIMIT)


def _is_ctx_rows(tile, bm, n_lat):
    rows = tile * bm + lax.broadcasted_iota(jnp.int32, (bm, 1), 0)
    return rows >= n_lat


def _sel(is_ctx, ref):
    return jnp.where(is_ctx, ref[1:2, :], ref[0:1, :])


def _ada_kernel(cc_ref, down_ref, up_ref, bias_ref, o_ref, t_ref):
    @pl.when(pl.program_id(1) == 0)
    def _():
        c = cc_ref[...]
        s = c * jax.nn.sigmoid(c)
        t = jnp.dot(s.astype(BF16), down_ref[...].astype(BF16), preferred_element_type=F32)
        t_ref[...] = t.astype(BF16)

    o_ref[...] = jnp.dot(t_ref[...], up_ref[...].astype(BF16), preferred_element_type=F32) + bias_ref[...]


def _ada_call(cc, down, up, bias):
    depth, d, rank = down.shape
    n = up.shape[-1]
    bn = _pick(n, (8192, 4096, 2048, 1024, 512, 256, 128))
    return pl.pallas_call(
        _ada_kernel,
        grid=(depth, n // bn),
        in_specs=[
            pl.BlockSpec((SUBLANES, d), lambda l, j: (0, 0)),
            pl.BlockSpec((None, d, rank), lambda l, j: (l, 0, 0)),
            pl.BlockSpec((None, rank, bn), lambda l, j: (l, 0, j)),
            pl.BlockSpec((None, 1, bn), lambda l, j: (l, 0, j)),
        ],
        out_specs=pl.BlockSpec((None, SUBLANES, bn), lambda l, j: (l, 0, j)),
        out_shape=jax.ShapeDtypeStruct((depth, SUBLANES, n), F32),
        scratch_shapes=[pltpu.VMEM((SUBLANES, rank), BF16)],
        compiler_params=_params("arbitrary", "arbitrary"),
        name="ada_mod",
    )(cc, down, up, bias)


NORM_ROWS = 16


def _norm_kernel(*refs, modulate, bm, n_lat):
    if modulate:
        x_ref, g_ref, sh_ref, sc_ref, o_ref = refs
    else:
        x_ref, g_ref, o_ref = refs
    g = g_ref[...]
    row0 = pl.program_id(0) * bm

    def step(r, _):
        rows = pl.ds(pl.multiple_of(r * NORM_ROWS, NORM_ROWS), NORM_ROWS)
        x = x_ref[rows, :].astype(F32)
        y = x * lax.rsqrt(jnp.mean(x * x, axis=-1, keepdims=True) + EPS)
        y = y * g
        if modulate:
            is_ctx = row0 + r * NORM_ROWS >= n_lat
            y = y * (1 + _sel(is_ctx, sc_ref)) + _sel(is_ctx, sh_ref)
        o_ref[rows, :] = y.astype(o_ref.dtype)
        return 0

    lax.fori_loop(0, bm // NORM_ROWS, step, 0, unroll=4)


def _norm_call(x, g, *, rows, out_dtype, mod=None, shift_blk=0, scale_blk=0, n_lat=0, name="rmsnorm"):
    k = x.shape[-1]
    bm = _pick(rows, (384, 320, 256, 128, 64, 32, 16, 8))
    in_specs = [pl.BlockSpec((bm, k), lambda i: (i, 0)), pl.BlockSpec((1, k), lambda i: (0, 0))]
    args = [x, g.reshape(1, k)]
    if mod is not None:
        in_specs += [pl.BlockSpec((SUBLANES, k), lambda i: (0, shift_blk)),
                     pl.BlockSpec((SUBLANES, k), lambda i: (0, scale_blk))]
        args += [mod, mod]
    return pl.pallas_call(
        functools.partial(_norm_kernel, modulate=mod is not None, bm=bm, n_lat=n_lat),
        grid=(rows // bm,),
        in_specs=in_specs,
        out_specs=pl.BlockSpec((bm, k), lambda i: (i, 0)),
        out_shape=jax.ShapeDtypeStruct((rows, k), out_dtype),
        compiler_params=_params("arbitrary"),
        name=name,
    )(*args)


def _rope(x, c, s1, s2):
    out = []
    for b in range(x.shape[-1] // LANES):
        xb = x[:, b * LANES:(b + 1) * LANES]
        up = pltpu.roll(xb, LANES - 32, 1)
        dn = pltpu.roll(xb, 32, 1)
        out.append(xb * c + up * s1 + dn * s2)
    return jnp.concatenate(out, axis=1)


def _rope_table_specs(bm, bn, rope_q_cols, row_col):
    q0, q1 = (0, 0) if rope_q_cols is None else (rope_q_cols[0] // bn, rope_q_cols[1] // bn)

    def index(*grid):
        i, j = row_col(*grid)
        return i, jnp.where((j >= q0) & (j < q1), 1, 0)

    return [pl.BlockSpec((bm, LANES), index)] * 3


def _mm_kernel(a_ref, w_ref, o_ref):
    o_ref[...] = jnp.dot(a_ref[...], w_ref[...], preferred_element_type=F32).astype(o_ref.dtype)


def _mm_call(a, w, *, out_dtype, name):
    m, k = a.shape
    n = w.shape[-1]
    bm = _pick(m, (768, 640, 512, 384, 256, 128))
    bn = _pick(n, (512, 896, 256, 128))
    return pl.pallas_call(
        _mm_kernel,
        grid=(m // bm, n // bn),
        in_specs=[pl.BlockSpec((bm, k), lambda i, j: (i, 0)), pl.BlockSpec((k, bn), lambda i, j: (0, j))],
        out_specs=pl.BlockSpec((bm, bn), lambda i, j: (i, j)),
        out_shape=jax.ShapeDtypeStruct((m, n), out_dtype),
        compiler_params=_params("arbitrary", "arbitrary"),
        name=name,
    )(a, w)


def _mm_res_kernel(a_ref, w_ref, res_ref, gate_ref, o_ref, *, bm, n_lat):
    acc = jnp.dot(a_ref[...], w_ref[...], preferred_element_type=F32)
    gate = _sel(_is_ctx_rows(pl.program_id(0), bm, n_lat), gate_ref)
    o_ref[...] = res_ref[...] + gate * acc


def _mm_res_call(a, w, res, mod, gate_blk, *, n_lat, name):
    m, n = res.shape
    k = a.shape[-1]
    bm = _pick(m, (768, 640, 384, 256, 128))
    bn = _pick(n, (256, 128))
    return pl.pallas_call(
        functools.partial(_mm_res_kernel, bm=bm, n_lat=n_lat),
        grid=(m // bm, n // bn),
        in_specs=[pl.BlockSpec((bm, k), lambda i, j: (i, 0)),
                  pl.BlockSpec((k, bn), lambda i, j: (0, j)),
                  pl.BlockSpec((bm, bn), lambda i, j: (i, j)),
                  pl.BlockSpec((SUBLANES, bn), lambda i, j: (0, gate_blk * (n // bn) + j))],
        out_specs=pl.BlockSpec((bm, bn), lambda i, j: (i, j)),
        out_shape=jax.ShapeDtypeStruct((m, n), F32),
        compiler_params=_params("arbitrary", "arbitrary"),
        name=name,
    )(a, w, res, mod)


CAST_ROWS = 256


def _cast_weights(w_refs, wb_refs):
    for w_ref, wb_ref in zip(w_refs, wb_refs):
        nrows = math.gcd(w_ref.shape[0], CAST_ROWS)

        def step(r, _, w_ref=w_ref, wb_ref=wb_ref, nrows=nrows):
            rows = pl.ds(pl.multiple_of(r * nrows, nrows), nrows)
            wb_ref[rows, :] = w_ref[rows, :].astype(BF16)
            return 0
        lax.fori_loop(0, w_ref.shape[0] // nrows, step, 0)


ROPE_ROW_CHUNKS = 4


def _wmm_kernel(*refs, parts, epi, bm, n_lat, rope_tiles, rope_all, side, out_scale, a_norm):
    n_extra = {"plain": 0, "transpose": 0, "rope": 3, "res": 2, "swiglu": 0}[epi]
    n_w = 2 * parts if epi == "swiglu" else parts
    a_refs, w_refs = refs[:parts], refs[parts:parts + n_w]
    extra = refs[parts + n_w:parts + n_w + n_extra]
    rest = refs[parts + n_w + n_extra:]
    if a_norm:
        gain_ref, rest = rest[0], rest[1:]
    if side:
        side_in, o_ref, side_out = rest[:3]
        wb_refs = rest[3:]
        side_out[...] = side_in[...].astype(BF16)
    else:
        o_ref, wb_refs = rest[0], rest[1:]
    j, i = pl.program_id(0), pl.program_id(1)

    @pl.when(i == 0)
    def _():
        _cast_weights(w_refs, wb_refs)

    def lhs(a_ref, rows):
        if not a_norm:
            return a_ref[rows, :]
        x = a_ref[rows, :].astype(F32)
        return (x * lax.rsqrt(jnp.mean(x * x, axis=-1, keepdims=True) + EPS) * gain_ref[...]).astype(BF16)

    def mm(wbs, rows=slice(None)):
        acc = jnp.dot(lhs(a_refs[0], rows), wbs[0][...], preferred_element_type=F32)
        for a_ref, wb_ref in zip(a_refs[1:], wbs[1:]):
            acc = acc + jnp.dot(lhs(a_ref, rows), wb_ref[...], preferred_element_type=F32)
        return acc

    if epi == "swiglu":
        g = mm(wb_refs[:parts])
        u = mm(wb_refs[parts:])
        o_ref[...] = (g * jax.nn.sigmoid(g) * u).astype(o_ref.dtype)
    elif epi == "res":
        res_ref, gate_ref = extra
        gate = _sel(_is_ctx_rows(i, bm, n_lat), gate_ref)
        o_ref[...] = res_ref[...] + gate * mm(wb_refs)
    elif epi == "rope" and rope_all:
        rc = bm // ROPE_ROW_CHUNKS
        pending = None
        for c in range(ROPE_ROW_CHUNKS):
            rows = slice(c * rc, (c + 1) * rc)
            acc = mm(wb_refs, rows)
            if pending is not None:
                pending()

            def pending(rows=rows, acc=acc):
                o_ref[rows, :] = _rope(acc, *(t[rows, :] for t in extra)).astype(o_ref.dtype)
        pending()
    elif epi == "rope":
        acc = mm(wb_refs)
        roped = (j >= rope_tiles[0]) & (j < rope_tiles[1])

        @pl.when(roped)
        def _():
            o_ref[...] = _rope(acc, *(t[...] for t in extra)).astype(o_ref.dtype)

        @pl.when(jnp.logical_not(roped))
        def _():
            o_ref[...] = acc.astype(o_ref.dtype)
    elif epi == "transpose":
        o_ref[...] = mm(wb_refs).T.astype(o_ref.dtype)
    else:
        o_ref[...] = (mm(wb_refs) * out_scale).astype(o_ref.dtype)


def _wmm_call(a_list, weights, *, n, col0=0, out_dtype, epi="plain", rope=None, rope_cols=None, rope_q_cols=None,
              res=None, mod=None,
              gate_blk=0, n_lat=0, bm_cands=(768, 640, 512, 384, 256, 128), bn_cands=(512, 256, 128),
              side_cast=None, out_scale=1.0, a_norm=None, name="wmm"):
    m = a_list[0].shape[0]
    parts = len(a_list)
    bm = _pick(m, bm_cands)
    bn = _pick(math.gcd(n, col0) if col0 else n, bn_cands)
    cb0 = col0 // bn
    in_specs = [pl.BlockSpec((bm, a.shape[1]), lambda j, i: (i, 0)) for a in a_list]
    args = list(a_list)
    scratch = []
    for idx, (w, layer, rb) in enumerate(weights):
        k = a_list[idx % parts].shape[1]
        if layer is None:
            in_specs.append(pl.BlockSpec((k, bn), lambda j, i, rb=rb: (rb, cb0 + j)))
        else:
            in_specs.append(pl.BlockSpec((None, k, bn), lambda j, i, layer=layer, rb=rb: (layer, rb, cb0 + j)))
        args.append(w)
        scratch.append(pltpu.VMEM((k, bn), BF16))
    rope_tiles, rope_all = None, False
    if epi == "rope":
        rope_tiles = (rope_cols[0] // bn, rope_cols[1] // bn)
        rope_all = rope_tiles == (0, n // bn) and bm % (ROPE_ROW_CHUNKS * BF16_ROWS) == 0
        in_specs += _rope_table_specs(bm, bn, rope_q_cols, lambda j, i: (i, j))
        args += list(rope)
    elif epi == "res":
        in_specs += [pl.BlockSpec((bm, bn), lambda j, i: (i, j)),
                     pl.BlockSpec((SUBLANES, bn), lambda j, i: (0, gate_blk * (n // bn) + j))]
        args += [res, mod]
    if a_norm is not None:
        in_specs.append(pl.BlockSpec((1, a_norm.shape[-1]), lambda j, i: (0, 0)))
        args.append(a_norm.reshape(1, -1))
    out_specs = (pl.BlockSpec((bn, bm), lambda j, i: (j, i)) if epi == "transpose"
                 else pl.BlockSpec((bm, bn), lambda j, i: (i, j)))
    out_shape = jax.ShapeDtypeStruct((n, m) if epi == "transpose" else (m, n), out_dtype)
    if side_cast is not None:
        w_side, side_layer = side_cast
        _, side_rows, side_cols = w_side.shape
        steps, ni = (n // bn) * (m // bm), m // bm
        slab = next(r for r in (64, 128, 256, 512, 1024, side_rows) if side_rows % r == 0 and side_rows // r <= steps)
        last = side_rows // slab - 1
        in_specs.append(pl.BlockSpec((None, slab, side_cols),
                                     lambda j, i: (side_layer, jnp.minimum(j * ni + i, last), 0)))
        args.append(w_side)
        out_specs = (out_specs, pl.BlockSpec((slab, side_cols), lambda j, i: (jnp.minimum(j * ni + i, last), 0)))
        out_shape = (out_shape, jax.ShapeDtypeStruct((side_rows, side_cols), BF16))
    return pl.pallas_call(
        functools.partial(_wmm_kernel, parts=parts, epi=epi, bm=bm, n_lat=n_lat, rope_tiles=rope_tiles,
                          rope_all=rope_all, side=side_cast is not None, out_scale=out_scale,
                          a_norm=a_norm is not None),
        grid=(n // bn, m // bm),
        in_specs=in_specs,
        out_specs=out_specs,
        out_shape=out_shape,
        scratch_shapes=scratch,
        compiler_params=_params("arbitrary", "arbitrary"),
        name=name,
    )(*args)


def _conv_kernel(bg_ref, cg_ref, xa_ref, cgp_ref, xap_ref, cgn_ref, xan_ref, w_ref, o_ref, *, bm, n_lat, m):
    i = pl.program_id(0)
    u = cg_ref[...] * xa_ref[...]
    u_prev = cgp_ref[SUBLANES - 1:SUBLANES, :] * xap_ref[SUBLANES - 1:SUBLANES, :]
    u_next = cgn_ref[0:1, :] * xan_ref[0:1, :]
    loc = lax.broadcasted_iota(jnp.int32, (bm, 1), 0)
    rows = i * bm + loc
    prev = jnp.where(loc == 0, u_prev, pltpu.roll(u, 1, 0))
    prev = jnp.where((rows == 0) | (rows == n_lat), 0.0, prev)
    nxt = jnp.where(loc == bm - 1, u_next, pltpu.roll(u, bm - 1, 0))
    nxt = jnp.where((rows == n_lat - 1) | (rows == m - 1), 0.0, nxt)
    y = w_ref[0:1, :] * prev + w_ref[1:2, :] * u + w_ref[2:3, :] * nxt
    o_ref[...] = (bg_ref[...] * y).astype(o_ref.dtype)


def _conv_call(pa, conv_w, *, n_lat):
    m = pa.shape[0]
    aw = conv_w.shape[-1]
    bm = _pick(math.gcd(n_lat, m - n_lat), (256, 128, 64, 32, 16, 8))
    bc = _pick(aw, (2048, 1024, 512, 256, 128))
    nc = aw // bc
    rb = bm // SUBLANES
    last = m // SUBLANES - 1
    main = lambda off: pl.BlockSpec((bm, bc), lambda i, c: (i, off * nc + c))
    prev = lambda off: pl.BlockSpec((SUBLANES, bc), lambda i, c: (jnp.maximum(i * rb - 1, 0), off * nc + c))
    nxt = lambda off: pl.BlockSpec((SUBLANES, bc), lambda i, c: (jnp.minimum((i + 1) * rb, last), off * nc + c))
    return pl.pallas_call(
        functools.partial(_conv_kernel, bm=bm, n_lat=n_lat, m=m),
        grid=(m // bm, nc),
        in_specs=[main(0), main(1), main(2), prev(1), prev(2), nxt(1), nxt(2),
                  pl.BlockSpec((conv_w.shape[0], bc), lambda i, c: (0, c))],
        out_specs=pl.BlockSpec((bm, bc), lambda i, c: (i, c)),
        out_shape=jax.ShapeDtypeStruct((m, aw), BF16),
        compiler_params=_params("arbitrary", "arbitrary"),
        name="short_conv",
    )(pa, pa, pa, pa, pa, pa, pa, conv_w)


def _pool_kernel(u_ref, up_ref, un_ref, w_ref, sc_ref, o_ref, *, bm, n_lat, m, pg):
    i = pl.program_id(0)
    ext = bm + 2 * POOL_HALO
    is_ctx = i * bm >= n_lat
    lo = jnp.where(is_ctx, n_lat, 0)
    hi = jnp.where(is_ctx, m, n_lat)
    rows_ext = i * bm - POOL_HALO + lax.broadcasted_iota(jnp.int32, (ext, 1), 0)
    valid = (rows_ext >= lo) & (rows_ext < hi)
    t = i * bm + lax.broadcasted_iota(jnp.int32, (bm, 1), 0) - lo
    t_len = hi - lo
    for g, win in enumerate(POOL_WINDOWS):
        cols = slice(g * pg, (g + 1) * pg)
        u = u_ref[:, cols]
        e = jnp.concatenate([up_ref[:, cols], u, un_ref[:, cols]], axis=0)
        e = jnp.where(valid, e, 0.0)
        span = 1
        while span < win:
            e = e + pltpu.roll(e, ext - span, 0)
            span *= 2
        start = POOL_HALO - win // 2
        if start:
            e = pltpu.roll(e, ext - start, 0)
        wsum = e[0:bm, :]
        w_lo = jnp.clip(t - win // 2, 0, t_len - 1)
        w_hi = jnp.clip(t + win // 2 - 1, 0, t_len - 1)
        cnt = (w_hi - w_lo + 1).astype(F32)
        diff = (wsum / cnt - u).astype(BF16)
        y = jnp.dot(diff, w_ref[g], preferred_element_type=F32) * sc_ref[:, cols]
        o_ref[:, cols] = y.astype(o_ref.dtype)


def _pool_call(u, pool_w, pool_scale, *, n_lat):
    m, pw = u.shape
    pg = pw // len(POOL_WINDOWS)
    bm = _pick(math.gcd(n_lat, m - n_lat), (256, 128, 64, 32, 16, 8))
    rb = bm // SUBLANES
    last = m // SUBLANES - 1
    return pl.pallas_call(
        functools.partial(_pool_kernel, bm=bm, n_lat=n_lat, m=m, pg=pg),
        grid=(m // bm,),
        in_specs=[pl.BlockSpec((bm, pw), lambda i: (i, 0)),
                  pl.BlockSpec((SUBLANES, pw), lambda i: (jnp.maximum(i * rb - 1, 0), 0)),
                  pl.BlockSpec((SUBLANES, pw), lambda i: (jnp.minimum((i + 1) * rb, last), 0)),
                  pl.BlockSpec(pool_w.shape, lambda i: (0, 0, 0)),
                  pl.BlockSpec((1, pw), lambda i: (0, 0))],
        out_specs=pl.BlockSpec((bm, pw), lambda i: (i, 0)),
        out_shape=jax.ShapeDtypeStruct((m, pw), BF16),
        compiler_params=_params("arbitrary"),
        name="multiscale_pool",
    )(u, u, u, pool_w, pool_scale.reshape(1, pw))


def _flash_tiles(q_tiles, k_rows, vt_ref, finish, *, chunks):
    dv = vt_ref.shape[0]

    nk = len(chunks)

    def update(t, s, state):
        m_run, l_run, acc = state
        m_new = jnp.maximum(m_run, jnp.max(s, axis=0, keepdims=True))
        alpha = jnp.exp2(m_run - m_new)
        p = jnp.exp2(s - m_new)
        l_new = alpha * l_run + jnp.sum(p, axis=0, keepdims=True)
        acc = alpha * acc + jnp.dot(vt_ref[:, chunks[t]], p.astype(BF16), preferred_element_type=F32)
        return m_new, l_new, acc

    pending = None
    for n, qs in enumerate(q_tiles):
        r = qs.shape[0]
        scores_of = lambda t, qs=qs: lax.dot_general(k_rows(chunks[t]), qs, (((1,), (1,)), ((), ())),
                                                     preferred_element_type=F32)
        state = (jnp.full((1, r), -jnp.inf, F32), jnp.zeros((1, r), F32), jnp.zeros((dv, r), F32))
        scores = scores_of(0)
        if pending is not None:
            pending()
        for t in range(nk - 1):
            nxt = scores_of(t + 1)
            state = update(t, scores, state)
            scores = nxt

        def pending(n=n, scores=scores, state=state):
            _, l_run, acc = update(nk - 1, scores, state)
            finish(n, l_run, acc)
    pending()


def _diff_attn_kernel(q_ref, k_ref, vt_ref, lq1_ref, lk1_ref, lq2_ref, lk2_ref, g_ref, *rest, bq, tiles, chunks,
                      lam_init):
    o_ref = rest[-1]
    bt = bq // tiles
    lane = lax.broadcasted_iota(jnp.int32, (bt, HEAD_DIM), 1)
    zero = jnp.zeros((bt, HEAD_DIM), q_ref.dtype)
    q_tiles = []
    for n in range(tiles):
        q = q_ref[n * bt:(n + 1) * bt, :]
        q_tiles.append(jnp.concatenate([jnp.where(lane < HEAD_DIM // 2, q, zero),
                                        jnp.where(lane >= HEAD_DIM // 2, q, zero)], axis=0))
    lam = (jnp.exp(jnp.sum(lq1_ref[...] * lk1_ref[...], axis=1, keepdims=True))
           - jnp.exp(jnp.sum(lq2_ref[...] * lk2_ref[...], axis=1, keepdims=True)) + lam_init)

    def finish(n, l_run, acc):
        o_t = acc / l_run
        o = (o_t[:, :bt] - lam * o_t[:, bt:]).T
        y = o * lax.rsqrt(jnp.mean(o * o, axis=-1, keepdims=True) + EPS)
        o_ref[n * bt:(n + 1) * bt, :] = (y * g_ref[...] * (1 - lam_init)).astype(o_ref.dtype)

    _flash_tiles(q_tiles, lambda rows: k_ref[rows, :], vt_ref, finish, chunks=chunks)


DIFF_CHUNK_SHARES = (2, 5, 6, 6, 6, 5, 3)
MLA_CHUNK_SHARES = (5, 10, 10, 8)


def _kv_chunks(kv_rows, shares):
    unit, total = MXU_DIM, sum(shares)
    if kv_rows % (unit * total):
        return (slice(0, kv_rows),)
    sizes = [kv_rows // total * share for share in shares]
    starts = [sum(sizes[:t]) for t in range(len(sizes))]
    return tuple(slice(a, a + n) for a, n in zip(starts, sizes))


def _out_alias(out, n_in):
    return jax.ShapeDtypeStruct(out.shape, out.dtype), [pl.BlockSpec(memory_space=pl.ANY)], [out], {n_in: 0}


def _diff_attn_call(qk, vt, lam_vecs, subln, out, *, heads, q_rows, q_row0, kv_rows, kv_row0, lam_init, name):
    bq = _pick(q_rows, (512, 256, 128))
    tiles = max(1, bq // 256)
    qb0, kb0 = q_row0 // bq, kv_row0 // kv_rows
    vec = pl.BlockSpec((1, HEAD_DIM // 2), lambda h, i: (0, 0))
    out_shape, alias_specs, alias_args, aliases = _out_alias(out, 8)
    return pl.pallas_call(
        functools.partial(_diff_attn_kernel, bq=bq, tiles=tiles, chunks=_kv_chunks(kv_rows, DIFF_CHUNK_SHARES), lam_init=lam_init),
        grid=(heads, q_rows // bq),
        in_specs=[pl.BlockSpec((bq, HEAD_DIM), lambda h, i: (qb0 + i, h)),
                  pl.BlockSpec((kv_rows, HEAD_DIM), lambda h, i: (kb0, heads + h)),
                  pl.BlockSpec((HEAD_DIM, kv_rows), lambda h, i: (h, kb0)),
                  vec, vec, vec, vec,
                  pl.BlockSpec((1, HEAD_DIM), lambda h, i: (0, 0))] + alias_specs,
        out_specs=pl.BlockSpec((bq, HEAD_DIM), lambda h, i: (qb0 + i, h)),
        out_shape=out_shape,
        input_output_aliases=aliases,
        compiler_params=_params("arbitrary", "arbitrary"),
        name=name,
    )(qk, qk, vt, *lam_vecs, subln, *alias_args)


def _mla_attn_kernel(qn_ref, qp_ref, kn_ref, kp_ref, vt_ref, *rest, bq, tiles, chunks):
    o_ref = rest[-1]
    bt = bq // tiles
    q_tiles = [jnp.concatenate([qn_ref[n * bt:(n + 1) * bt, :], qp_ref[n * bt:(n + 1) * bt, :]], axis=1)
               for n in range(tiles)]

    def k_rows(rows):
        return jnp.concatenate([kn_ref[rows, :], kp_ref[rows, :]], axis=1)

    def finish(n, l_run, acc):
        o_ref[n * bt:(n + 1) * bt, :] = (acc / l_run).T.astype(o_ref.dtype)

    _flash_tiles(q_tiles, k_rows, vt_ref, finish, chunks=chunks)


def _mla_attn_call(qn, qp, kn, kpe, vt, out, *, heads, q_rows, q_row0, kv_rows, kv_row0, name):
    bq = _pick(q_rows, (1024, 512, 256, 128))
    tiles = max(1, bq // 512)
    qb0, kb0 = q_row0 // bq, kv_row0 // kv_rows
    out_shape, alias_specs, alias_args, aliases = _out_alias(out, 5)
    return pl.pallas_call(
        functools.partial(_mla_attn_kernel, bq=bq, tiles=tiles, chunks=_kv_chunks(kv_rows, MLA_CHUNK_SHARES)),
        grid=(heads, q_rows // bq),
        in_specs=[pl.BlockSpec((bq, HEAD_DIM), lambda h, i: (qb0 + i, h)),
                  pl.BlockSpec((bq, HEAD_DIM), lambda h, i: (qb0 + i, h)),
                  pl.BlockSpec((kv_rows, HEAD_DIM), lambda h, i: (kb0, h)),
                  pl.BlockSpec((kv_rows, LANES), lambda h, i: (kb0, 0)),
                  pl.BlockSpec((HEAD_DIM, kv_rows), lambda h, i: (h, kb0))] + alias_specs,
        out_specs=pl.BlockSpec((bq, HEAD_DIM), lambda h, i: (qb0 + i, h)),
        out_shape=out_shape,
        input_output_aliases=aliases,
        compiler_params=_params("arbitrary", "arbitrary"),
        name=name,
    )(qn, qp, kn, kpe, vt, *alias_args)


def _rope_tables(seq, ctx_len, q_scale):
    rows = seq // GRID_W
    row = jnp.broadcast_to(jnp.arange(rows, dtype=F32)[:, None], (rows, GRID_W)).reshape(seq)
    col = jnp.broadcast_to(jnp.arange(GRID_W, dtype=F32)[None, :], (rows, GRID_W)).reshape(seq)
    n_freq = 16
    inv_freq = ROPE_BASE ** (-jnp.arange(n_freq, dtype=F32) / n_freq)
    ang = jnp.concatenate([row[:, None] * inv_freq, col[:, None] * inv_freq], axis=-1)
    cos, sin = jnp.cos(ang), jnp.sin(ang)
    zero = jnp.zeros_like(sin)
    c = jnp.concatenate([cos, cos, cos, cos], axis=-1)
    s1 = jnp.concatenate([-sin, zero, -sin, zero], axis=-1)
    s2 = jnp.concatenate([zero, sin, zero, sin], axis=-1)
    pad = lambda t, v: jnp.concatenate([t, jnp.full((ctx_len, LANES), v, F32)], axis=0)
    return tuple(jnp.concatenate([t, t * q_scale], axis=1) for t in (pad(c, 1.0), pad(s1, 0.0), pad(s2, 0.0)))


def kernel(x, c, ctx, c_ctx, ada_down, ada_up, ada_bias, norm_mix, norm_ffn, ffn_gate, ffn_up, ffn_down, ev_w_in, ev_conv, ev_lam_q1, ev_lam_k1, ev_lam_q2, ev_lam_k2, ev_subln, ev_w_out, od_w_in, od_q_norm, od_w_uq, od_kv_norm, od_w_ukv, od_pool_w, od_pool_scale, od_w_out, final_norm):
    _, seq, d = x.shape
    ctx_len = ctx.shape[1]
    m = seq + ctx_len
    depth = ada_down.shape[0]
    half = d // 2
    heads = half // HEAD_DIM
    q_rank = od_q_norm.shape[-1]
    kv_rank = od_kv_norm.shape[-1]
    rope_dim = HEAD_DIM // 2
    bf = lambda t: t.astype(BF16)

    r = jnp.concatenate([x[0], ctx[0]], axis=0)
    diff_q_scale = (HEAD_DIM // 2) ** -0.5 * LOG2E
    mla_q_scale = (HEAD_DIM + rope_dim) ** -0.5 * LOG2E
    tables_diff = _rope_tables(seq, ctx_len, diff_q_scale)
    tables_mla = _rope_tables(seq, ctx_len, mla_q_scale)

    cc = jnp.zeros((SUBLANES, d), F32).at[0].set(c[0]).at[1].set(c_ctx)
    mods = _ada_call(cc, ada_down, ada_up, ada_bias.reshape(depth, 1, -1))

    for l in range(depth):
        mod = mods[l]
        i = l // 2
        h = _norm_call(r, norm_mix[l], rows=m, out_dtype=BF16, mod=mod, shift_blk=0, scale_blk=1, n_lat=seq,
                       name="norm_mod_mix")
        if l % 2 == 0:
            lam_init = 0.8 - 0.6 * math.exp(-0.3 * l)
            big_rows = (1408, 1280, 768, 640, 512, 256, 128)
            pa = _wmm_call([h], [(ev_w_in, i, 0)], n=3 * half, out_dtype=F32, bm_cands=big_rows, name="ev_in_conv")
            qk = _wmm_call([h], [(ev_w_in, i, 0)], n=2 * half, col0=3 * half, out_dtype=BF16, epi="rope",
                           rope=tables_diff, rope_cols=(0, 2 * half), rope_q_cols=(0, half), bm_cands=big_rows,
                           name="ev_in_qk")
            vt = _wmm_call([h], [(ev_w_in, i, 0)], n=half, col0=5 * half, out_dtype=BF16, epi="transpose",
                           bm_cands=big_rows, name="ev_in_vt")
            ya = _conv_call(pa, ev_conv[i], n_lat=seq)
            lam_vecs = [v[i].reshape(1, -1) for v in (ev_lam_q1, ev_lam_k1, ev_lam_q2, ev_lam_k2)]
            subln = ev_subln[i].reshape(1, -1)
            yb = jnp.zeros((m, half), BF16)
            yb = _diff_attn_call(qk, vt, lam_vecs, subln, yb, heads=heads, q_rows=seq, q_row0=0, kv_rows=m,
                                 kv_row0=0, lam_init=lam_init, name="diff_attn")
            yb = _diff_attn_call(qk, vt, lam_vecs, subln, yb, heads=heads, q_rows=ctx_len, q_row0=seq,
                                 kv_rows=ctx_len, kv_row0=seq, lam_init=lam_init, name="diff_attn_ctx")
            y1, y2 = ya, yb
            w_out = ev_w_out
        else:
            w_in = od_w_in[i]
            w_kpe = jnp.pad(w_in[:, q_rank + kv_rank:q_rank + kv_rank + rope_dim], ((0, 0), (0, LANES - rope_dim)))
            cq = _mm_call(h, bf(w_in[:, :q_rank]), out_dtype=F32, name="od_in_cq")
            ckv = _mm_call(h, bf(w_in[:, q_rank:q_rank + kv_rank]), out_dtype=F32, name="od_in_ckv")
            kpe = _wmm_call([h], [(w_kpe, None, 0)], n=LANES, out_dtype=BF16, epi="rope", rope=tables_mla,
                            rope_cols=(0, LANES), name="od_in_kpe")
            u = _mm_call(h, bf(w_in[:, q_rank + kv_rank + rope_dim:]), out_dtype=F32, name="od_in_pool")
            w_uq = od_w_uq[i].reshape(q_rank, heads, HEAD_DIM + rope_dim)
            w_qn = w_uq[:, :, :HEAD_DIM].reshape(q_rank, half)
            w_qp = jnp.pad(w_uq[:, :, HEAD_DIM:], ((0, 0), (0, 0), (0, LANES - rope_dim))).reshape(q_rank, heads * LANES)
            wide = (2048, 1024, 512, 256, 128)
            qn = _wmm_call([cq], [(w_qn, None, 0)], n=half, out_dtype=BF16, out_scale=mla_q_scale, bn_cands=wide,
                           a_norm=od_q_norm[i], name="od_uq_nope")
            qp = _wmm_call([cq], [(w_qp, None, 0)], n=heads * LANES, out_dtype=BF16, epi="rope", rope=tables_mla,
                           rope_cols=(0, heads * LANES), rope_q_cols=(0, heads * LANES), bn_cands=wide,
                           a_norm=od_q_norm[i], name="od_uq_rope")
            w_ukv = od_w_ukv[i].reshape(kv_rank, heads, 2 * HEAD_DIM)
            kn = _wmm_call([ckv], [(w_ukv[:, :, :HEAD_DIM].reshape(kv_rank, half), None, 0)], n=half, out_dtype=BF16,
                           bn_cands=wide, a_norm=od_kv_norm[i], name="od_uk")
            vt = _wmm_call([ckv], [(w_ukv[:, :, HEAD_DIM:].reshape(kv_rank, half), None, 0)], n=half, out_dtype=BF16,
                           epi="transpose", bn_cands=wide, a_norm=od_kv_norm[i], name="od_uvt")
            ym = jnp.zeros((m, half), BF16)
            ym = _mla_attn_call(qn, qp, kn, kpe, vt, ym, heads=heads, q_rows=seq, q_row0=0, kv_rows=m, kv_row0=0,
                                name="mla_attn")
            ym = _mla_attn_call(qn, qp, kn, kpe, vt, ym, heads=heads, q_rows=ctx_len, q_row0=seq, kv_rows=ctx_len,
                                kv_row0=seq, name="mla_attn_ctx")
            y1 = _pool_call(u, bf(od_pool_w[i]), od_pool_scale[i], n_lat=seq)
            y2 = ym
            w_out = od_w_out
        r = _wmm_call([y1, y2], [(w_out, i, 0), (w_out, i, 1)], n=d, out_dtype=F32, epi="res", res=r, mod=mod,
                      gate_blk=2, n_lat=seq, name="mix_out")
        h2 = _norm_call(r, norm_ffn[l], rows=m, out_dtype=BF16, mod=mod, shift_blk=3, scale_blk=4, n_lat=seq,
                        name="norm_mod_ffn")
        a, w_down = _wmm_call([h2], [(ffn_gate, l, 0), (ffn_up, l, 0)], n=ffn_gate.shape[-1], out_dtype=BF16,
                              epi="swiglu", bm_cands=(1408, 1280, 768, 640, 512, 256, 128), bn_cands=(256, 128),
                              side_cast=(ffn_down, l), name="ffn_gate_up")
        r = _mm_res_call(a, w_down, r, mod, 5, n_lat=seq, name="ffn_down")
    out = _norm_call(r, final_norm, rows=seq, out_dtype=F32, name="final_norm")
    return out[None]
```

```python
import functools
import math

import jax
import jax.numpy as jnp
from jax import lax
from jax.experimental import pallas as pl
from jax.experimental.pallas import tpu as pltpu

F32 = jnp.float32
BF16 = jnp.bfloat16

LANES = 128
SUBLANES = 8
BF16_ROWS = 16
HEAD_DIM = 128
MXU_DIM = 256
GRID_W = 64
ROPE_BASE = 10000.0
EPS = 1e-6
POOL_WINDOWS = (2, 4, 8, 16)
POOL_HALO = 8
LOG2E = 1.4426950408889634
VMEM_LIMIT = 56 * 1024 * 1024


def _pick(n, cands):
    for c in cands:
        if c <= n and n % c == 0:
            return c
    return n


def _params(*sem):
    return pltpu.CompilerParams(dimension_semantics=sem, vmem_limit_bytes=VMEM_LIMIT)


def _is_ctx_rows(tile, bm, n_lat):
    rows = tile * bm + lax.broadcasted_iota(jnp.int32, (bm, 1), 0)
    return rows >= n_lat


def _sel(is_ctx, ref):
    return jnp.where(is_ctx, ref[1:2, :], ref[0:1, :])


def _ada_kernel(cc_ref, down_ref, up_ref, bias_ref, o_ref, t_ref):
    @pl.when(pl.program_id(1) == 0)
    def _():
        c = cc_ref[...]
        s = c * jax.nn.sigmoid(c)
        t = jnp.dot(s.astype(BF16), down_ref[...].astype(BF16), preferred_element_type=F32)
        t_ref[...] = t.astype(BF16)

    o_ref[...] = jnp.dot(t_ref[...], up_ref[...].astype(BF16), preferred_element_type=F32) + bias_ref[...]


def _ada_call(cc, down, up, bias):
    depth, d, rank = down.shape
    n = up.shape[-1]
    bn = _pick(n, (8192, 4096, 2048, 1024, 512, 256, 128))
    return pl.pallas_call(
        _ada_kernel,
        grid=(depth, n // bn),
        in_specs=[
            pl.BlockSpec((SUBLANES, d), lambda l, j: (0, 0)),
            pl.BlockSpec((None, d, rank), lambda l, j: (l, 0, 0)),
            pl.BlockSpec((None, rank, bn), lambda l, j: (l, 0, j)),
            pl.BlockSpec((None, 1, bn), lambda l, j: (l, 0, j)),
        ],
        out_specs=pl.BlockSpec((None, SUBLANES, bn), lambda l, j: (l, 0, j)),
        out_shape=jax.ShapeDtypeStruct((depth, SUBLANES, n), F32),
        scratch_shapes=[pltpu.VMEM((SUBLANES, rank), BF16)],
        compiler_params=_params("arbitrary", "arbitrary"),
        name="ada_mod",
    )(cc, down, up, bias)


NORM_ROWS = 16


def _norm_kernel(*refs, modulate, bm, n_lat):
    if modulate:
        x_ref, g_ref, sh_ref, sc_ref, o_ref = refs
    else:
        x_ref, g_ref, o_ref = refs
    g = g_ref[...]
    row0 = pl.program_id(0) * bm

    def step(r, _):
        rows = pl.ds(pl.multiple_of(r * NORM_ROWS, NORM_ROWS), NORM_ROWS)
        x = x_ref[rows, :].astype(F32)
        y = x * lax.rsqrt(jnp.mean(x * x, axis=-1, keepdims=True) + EPS)
        y = y * g
        if modulate:
            is_ctx = row0 + r * NORM_ROWS >= n_lat
            y = y * (1 + _sel(is_ctx, sc_ref)) + _sel(is_ctx, sh_ref)
        o_ref[rows, :] = y.astype(o_ref.dtype)
        return 0

    lax.fori_loop(0, bm // NORM_ROWS, step, 0, unroll=4)


def _norm_join_kernel(x_ref, c_ref, g_ref, sh_ref, sc_ref, h_ref, r_ref, *, bm, nx):
    g = g_ref[...]

    def run(src_ref, row):
        scale1 = 1 + sc_ref[row:row + 1, :]
        shift = sh_ref[row:row + 1, :]

        def step(r, _):
            rows = pl.ds(pl.multiple_of(r * NORM_ROWS, NORM_ROWS), NORM_ROWS)
            x = src_ref[rows, :]
            r_ref[rows, :] = x
            y = x * lax.rsqrt(jnp.mean(x * x, axis=-1, keepdims=True) + EPS)
            h_ref[rows, :] = (y * g * scale1 + shift).astype(h_ref.dtype)
            return 0

        lax.fori_loop(0, bm // NORM_ROWS, step, 0, unroll=4)

    @pl.when(pl.program_id(0) < nx)
    def _():
        run(x_ref, 0)

    @pl.when(pl.program_id(0) >= nx)
    def _():
        run(c_ref, 1)


def _norm_join_call(x, ctx, g, mod, *, shift_blk, scale_blk):
    seq, k = x.shape
    ctx_len = ctx.shape[0]
    bm = _pick(math.gcd(seq, ctx_len), (256, 128, 64, 32, 16))
    nx, m = seq // bm, seq + ctx_len
    return pl.pallas_call(
        functools.partial(_norm_join_kernel, bm=bm, nx=nx),
        grid=(m // bm,),
        in_specs=[pl.BlockSpec((bm, k), lambda i: (jnp.minimum(i, nx - 1), 0)),
                  pl.BlockSpec((bm, k), lambda i: (jnp.maximum(i - nx, 0), 0)),
                  pl.BlockSpec((1, k), lambda i: (0, 0)),
                  pl.BlockSpec((SUBLANES, k), lambda i: (0, shift_blk)),
                  pl.BlockSpec((SUBLANES, k), lambda i: (0, scale_blk))],
        out_specs=(pl.BlockSpec((bm, k), lambda i: (i, 0)), pl.BlockSpec((bm, k), lambda i: (i, 0))),
        out_shape=(jax.ShapeDtypeStruct((m, k), BF16), jax.ShapeDtypeStruct((m, k), F32)),
        compiler_params=_params("arbitrary"),
        name="norm_mod_join",
    )(x, ctx, g.reshape(1, k), mod, mod)


def _norm_call(x, g, *, rows, out_dtype, mod=None, shift_blk=0, scale_blk=0, n_lat=0, name="rmsnorm"):
    k = x.shape[-1]
    bm = _pick(rows, (384, 320, 256, 128, 64, 32, 16, 8))
    in_specs = [pl.BlockSpec((bm, k), lambda i: (i, 0)), pl.BlockSpec((1, k), lambda i: (0, 0))]
    args = [x, g.reshape(1, k)]
    if mod is not None:
        in_specs += [pl.BlockSpec((SUBLANES, k), lambda i: (0, shift_blk)),
                     pl.BlockSpec((SUBLANES, k), lambda i: (0, scale_blk))]
        args += [mod, mod]
    return pl.pallas_call(
        functools.partial(_norm_kernel, modulate=mod is not None, bm=bm, n_lat=n_lat),
        grid=(rows // bm,),
        in_specs=in_specs,
        out_specs=pl.BlockSpec((bm, k), lambda i: (i, 0)),
        out_shape=jax.ShapeDtypeStruct((rows, k), out_dtype),
        compiler_params=_params("arbitrary"),
        name=name,
    )(*args)


def _rope(x, c, s1, s2):
    out = []
    for b in range(x.shape[-1] // LANES):
        xb = x[:, b * LANES:(b + 1) * LANES]
        up = pltpu.roll(xb, LANES - 32, 1)
        dn = pltpu.roll(xb, 32, 1)
        out.append(xb * c + up * s1 + dn * s2)
    return jnp.concatenate(out, axis=1)


def _rope_table_specs(bm, bn, rope_q_cols, row_col):
    q0, q1 = (0, 0) if rope_q_cols is None else (rope_q_cols[0] // bn, rope_q_cols[1] // bn)

    def index(*grid):
        i, j = row_col(*grid)
        return i, jnp.where((j >= q0) & (j < q1), 1, 0)

    return [pl.BlockSpec((bm, LANES), index)] * 3


def _mm_kernel(a_ref, w_ref, o_ref):
    o_ref[...] = jnp.dot(a_ref[...], w_ref[...], preferred_element_type=F32).astype(o_ref.dtype)


def _mm_call(a, w, *, out_dtype, name):
    m, k = a.shape
    n = w.shape[-1]
    bm = _pick(m, (768, 640, 512, 384, 256, 128))
    bn = _pick(n, (512, 896, 256, 128))
    return pl.pallas_call(
        _mm_kernel,
        grid=(m // bm, n // bn),
        in_specs=[pl.BlockSpec((bm, k), lambda i, j: (i, 0)), pl.BlockSpec((k, bn), lambda i, j: (0, j))],
        out_specs=pl.BlockSpec((bm, bn), lambda i, j: (i, j)),
        out_shape=jax.ShapeDtypeStruct((m, n), out_dtype),
        compiler_params=_params("arbitrary", "arbitrary"),
        name=name,
    )(a, w)


def _mm_res_kernel(a_ref, w_ref, res_ref, gate_ref, o_ref, *, bm, n_lat):
    acc = jnp.dot(a_ref[...], w_ref[...], preferred_element_type=F32)
    gate = _sel(_is_ctx_rows(pl.program_id(0), bm, n_lat), gate_ref)
    o_ref[...] = res_ref[...] + gate * acc


def _mm_res_call(a, w, res, mod, gate_blk, *, n_lat, name):
    m, n = res.shape
    k = a.shape[-1]
    bm = _pick(m, (768, 640, 384, 256, 128))
    bn = _pick(n, (256, 128))
    return pl.pallas_call(
        functools.partial(_mm_res_kernel, bm=bm, n_lat=n_lat),
        grid=(m // bm, n // bn),
        in_specs=[pl.BlockSpec((bm, k), lambda i, j: (i, 0)),
                  pl.BlockSpec((k, bn), lambda i, j: (0, j)),
                  pl.BlockSpec((bm, bn), lambda i, j: (i, j)),
                  pl.BlockSpec((SUBLANES, bn), lambda i, j: (0, gate_blk * (n // bn) + j))],
        out_specs=pl.BlockSpec((bm, bn), lambda i, j: (i, j)),
        out_shape=jax.ShapeDtypeStruct((m, n), F32),
        compiler_params=_params("arbitrary", "arbitrary"),
        name=name,
    )(a, w, res, mod)


CAST_ROWS = 256


def _cast_weights(w_refs, wb_refs):
    for w_ref, wb_ref in zip(w_refs, wb_refs):
        nrows = math.gcd(w_ref.shape[0], CAST_ROWS)

        def step(r, _, w_ref=w_ref, wb_ref=wb_ref, nrows=nrows):
            rows = pl.ds(pl.multiple_of(r * nrows, nrows), nrows)
            wb_ref[rows, :] = w_ref[rows, :].astype(BF16)
            return 0
        lax.fori_loop(0, w_ref.shape[0] // nrows, step, 0)


ROPE_ROW_CHUNKS = 4


def _wmm_kernel(*refs, parts, epi, bm, n_lat, rope_tiles, rope_all, side, out_scale, a_norm):
    n_extra = {"plain": 0, "transpose": 0, "rope": 3, "res": 2, "swiglu": 0}[epi]
    n_w = 2 * parts if epi == "swiglu" else parts
    a_refs, w_refs = refs[:parts], refs[parts:parts + n_w]
    extra = refs[parts + n_w:parts + n_w + n_extra]
    rest = refs[parts + n_w + n_extra:]
    if a_norm:
        gain_ref, rest = rest[0], rest[1:]
    if side:
        side_in, o_ref, side_out = rest[:3]
        wb_refs = rest[3:]
        side_out[...] = side_in[...].astype(BF16)
    else:
        o_ref, wb_refs = rest[0], rest[1:]
    j, i = pl.program_id(0), pl.program_id(1)

    @pl.when(i == 0)
    def _():
        _cast_weights(w_refs, wb_refs)

    def lhs(a_ref, rows):
        if not a_norm:
            return a_ref[rows, :]
        x = a_ref[rows, :].astype(F32)
        return (x * lax.rsqrt(jnp.mean(x * x, axis=-1, keepdims=True) + EPS) * gain_ref[...]).astype(BF16)

    def mm(wbs, rows=slice(None)):
        acc = jnp.dot(lhs(a_refs[0], rows), wbs[0][...], preferred_element_type=F32)
        for a_ref, wb_ref in zip(a_refs[1:], wbs[1:]):
            acc = acc + jnp.dot(lhs(a_ref, rows), wb_ref[...], preferred_element_type=F32)
        return acc

    if epi == "swiglu":
        g = mm(wb_refs[:parts])
        u = mm(wb_refs[parts:])
        o_ref[...] = (g * jax.nn.sigmoid(g) * u).astype(o_ref.dtype)
    elif epi == "res":
        res_ref, gate_ref = extra
        gate = _sel(_is_ctx_rows(i, bm, n_lat), gate_ref)
        o_ref[...] = res_ref[...] + gate * mm(wb_refs)
    elif epi == "rope" and rope_all:
        rc = bm // ROPE_ROW_CHUNKS
        pending = None
        for c in range(ROPE_ROW_CHUNKS):
            rows = slice(c * rc, (c + 1) * rc)
            acc = mm(wb_refs, rows)
            if pending is not None:
                pending()

            def pending(rows=rows, acc=acc):
                o_ref[rows, :] = _rope(acc, *(t[rows, :] for t in extra)).astype(o_ref.dtype)
        pending()
    elif epi == "rope":
        acc = mm(wb_refs)
        roped = (j >= rope_tiles[0]) & (j < rope_tiles[1])

        @pl.when(roped)
        def _():
            o_ref[...] = _rope(acc, *(t[...] for t in extra)).astype(o_ref.dtype)

        @pl.when(jnp.logical_not(roped))
        def _():
            o_ref[...] = acc.astype(o_ref.dtype)
    elif epi == "transpose":
        o_ref[...] = mm(wb_refs).T.astype(o_ref.dtype)
    else:
        o_ref[...] = (mm(wb_refs) * out_scale).astype(o_ref.dtype)


def _wmm_call(a_list, weights, *, n, col0=0, out_dtype, epi="plain", rope=None, rope_cols=None, rope_q_cols=None,
              res=None, mod=None,
              gate_blk=0, n_lat=0, bm_cands=(768, 640, 512, 384, 256, 128), bn_cands=(512, 256, 128),
              side_cast=None, out_scale=1.0, a_norm=None, name="wmm"):
    m = a_list[0].shape[0]
    parts = len(a_list)
    bm = _pick(m, bm_cands)
    bn = _pick(math.gcd(n, col0) if col0 else n, bn_cands)
    cb0 = col0 // bn
    in_specs = [pl.BlockSpec((bm, a.shape[1]), lambda j, i: (i, 0)) for a in a_list]
    args = list(a_list)
    scratch = []
    for idx, (w, layer, rb) in enumerate(weights):
        k = a_list[idx % parts].shape[1]
        if layer is None:
            in_specs.append(pl.BlockSpec((k, bn), lambda j, i, rb=rb: (rb, cb0 + j)))
        else:
            in_specs.append(pl.BlockSpec((None, k, bn), lambda j, i, layer=layer, rb=rb: (layer, rb, cb0 + j)))
        args.append(w)
        scratch.append(pltpu.VMEM((k, bn), BF16))
    rope_tiles, rope_all = None, False
    if epi == "rope":
        rope_tiles = (rope_cols[0] // bn, rope_cols[1] // bn)
        rope_all = rope_tiles == (0, n // bn) and bm % (ROPE_ROW_CHUNKS * BF16_ROWS) == 0
        in_specs += _rope_table_specs(bm, bn, rope_q_cols, lambda j, i: (i, j))
        args += list(rope)
    elif epi == "res":
        in_specs += [pl.BlockSpec((bm, bn), lambda j, i: (i, j)),
                     pl.BlockSpec((SUBLANES, bn), lambda j, i: (0, gate_blk * (n // bn) + j))]
        args += [res, mod]
    if a_norm is not None:
        in_specs.append(pl.BlockSpec((1, a_norm.shape[-1]), lambda j, i: (0, 0)))
        args.append(a_norm.reshape(1, -1))
    out_specs = (pl.BlockSpec((bn, bm), lambda j, i: (j, i)) if epi == "transpose"
                 else pl.BlockSpec((bm, bn), lambda j, i: (i, j)))
    out_shape = jax.ShapeDtypeStruct((n, m) if epi == "transpose" else (m, n), out_dtype)
    if side_cast is not None:
        w_side, side_layer = side_cast
        _, side_rows, side_cols = w_side.shape
        steps, ni = (n // bn) * (m // bm), m // bm
        slab = next(r for r in (64, 128, 256, 512, 1024, side_rows) if side_rows % r == 0 and side_rows // r <= steps)
        last = side_rows // slab - 1
        in_specs.append(pl.BlockSpec((None, slab, side_cols),
                                     lambda j, i: (side_layer, jnp.minimum(j * ni + i, last), 0)))
        args.append(w_side)
        out_specs = (out_specs, pl.BlockSpec((slab, side_cols), lambda j, i: (jnp.minimum(j * ni + i, last), 0)))
        out_shape = (out_shape, jax.ShapeDtypeStruct((side_rows, side_cols), BF16))
    return pl.pallas_call(
        functools.partial(_wmm_kernel, parts=parts, epi=epi, bm=bm, n_lat=n_lat, rope_tiles=rope_tiles,
                          rope_all=rope_all, side=side_cast is not None, out_scale=out_scale,
                          a_norm=a_norm is not None),
        grid=(n // bn, m // bm),
        in_specs=in_specs,
        out_specs=out_specs,
        out_shape=out_shape,
        scratch_shapes=scratch,
        compiler_params=_params("arbitrary", "arbitrary"),
        name=name,
    )(*args)


def _conv_kernel(bg_ref, cg_ref, xa_ref, cgp_ref, xap_ref, cgn_ref, xan_ref, w_ref, o_ref, *, bm, n_lat, m):
    i = pl.program_id(0)
    u = cg_ref[...] * xa_ref[...]
    u_prev = cgp_ref[SUBLANES - 1:SUBLANES, :] * xap_ref[SUBLANES - 1:SUBLANES, :]
    u_next = cgn_ref[0:1, :] * xan_ref[0:1, :]
    loc = lax.broadcasted_iota(jnp.int32, (bm, 1), 0)
    rows = i * bm + loc
    prev = jnp.where(loc == 0, u_prev, pltpu.roll(u, 1, 0))
    prev = jnp.where((rows == 0) | (rows == n_lat), 0.0, prev)
    nxt = jnp.where(loc == bm - 1, u_next, pltpu.roll(u, bm - 1, 0))
    nxt = jnp.where((rows == n_lat - 1) | (rows == m - 1), 0.0, nxt)
    y = w_ref[0:1, :] * prev + w_ref[1:2, :] * u + w_ref[2:3, :] * nxt
    o_ref[...] = (bg_ref[...] * y).astype(o_ref.dtype)


def _conv_call(pa, conv_w, *, n_lat):
    m = pa.shape[0]
    aw = conv_w.shape[-1]
    bm = _pick(math.gcd(n_lat, m - n_lat), (256, 128, 64, 32, 16, 8))
    bc = _pick(aw, (2048, 1024, 512, 256, 128))
    nc = aw // bc
    rb = bm // SUBLANES
    last = m // SUBLANES - 1
    main = lambda off: pl.BlockSpec((bm, bc), lambda i, c: (i, off * nc + c))
    prev = lambda off: pl.BlockSpec((SUBLANES, bc), lambda i, c: (jnp.maximum(i * rb - 1, 0), off * nc + c))
    nxt = lambda off: pl.BlockSpec((SUBLANES, bc), lambda i, c: (jnp.minimum((i + 1) * rb, last), off * nc + c))
    return pl.pallas_call(
        functools.partial(_conv_kernel, bm=bm, n_lat=n_lat, m=m),
        grid=(m // bm, nc),
        in_specs=[main(0), main(1), main(2), prev(1), prev(2), nxt(1), nxt(2),
                  pl.BlockSpec((conv_w.shape[0], bc), lambda i, c: (0, c))],
        out_specs=pl.BlockSpec((bm, bc), lambda i, c: (i, c)),
        out_shape=jax.ShapeDtypeStruct((m, aw), BF16),
        compiler_params=_params("arbitrary", "arbitrary"),
        name="short_conv",
    )(pa, pa, pa, pa, pa, pa, pa, conv_w)


def _pool_kernel(u_ref, up_ref, un_ref, w_ref, sc_ref, o_ref, *, bm, n_lat, m, pg):
    i = pl.program_id(0)
    ext = bm + 2 * POOL_HALO
    is_ctx = i * bm >= n_lat
    lo = jnp.where(is_ctx, n_lat, 0)
    hi = jnp.where(is_ctx, m, n_lat)
    rows_ext = i * bm - POOL_HALO + lax.broadcasted_iota(jnp.int32, (ext, 1), 0)
    valid = (rows_ext >= lo) & (rows_ext < hi)
    t = i * bm + lax.broadcasted_iota(jnp.int32, (bm, 1), 0) - lo
    t_len = hi - lo
    for g, win in enumerate(POOL_WINDOWS):
        cols = slice(g * pg, (g + 1) * pg)
        u = u_ref[:, cols]
        e = jnp.concatenate([up_ref[:, cols], u, un_ref[:, cols]], axis=0)
        e = jnp.where(valid, e, 0.0)
        span = 1
        while span < win:
            e = e + pltpu.roll(e, ext - span, 0)
            span *= 2
        start = POOL_HALO - win // 2
        if start:
            e = pltpu.roll(e, ext - start, 0)
        wsum = e[0:bm, :]
        w_lo = jnp.clip(t - win // 2, 0, t_len - 1)
        w_hi = jnp.clip(t + win // 2 - 1, 0, t_len - 1)
        cnt = (w_hi - w_lo + 1).astype(F32)
        diff = (wsum / cnt - u).astype(BF16)
        y = jnp.dot(diff, w_ref[g], preferred_element_type=F32) * sc_ref[:, cols]
        o_ref[:, cols] = y.astype(o_ref.dtype)


def _pool_call(u, pool_w, pool_scale, *, n_lat):
    m, pw = u.shape
    pg = pw // len(POOL_WINDOWS)
    bm = _pick(math.gcd(n_lat, m - n_lat), (256, 128, 64, 32, 16, 8))
    rb = bm // SUBLANES
    last = m // SUBLANES - 1
    return pl.pallas_call(
        functools.partial(_pool_kernel, bm=bm, n_lat=n_lat, m=m, pg=pg),
        grid=(m // bm,),
        in_specs=[pl.BlockSpec((bm, pw), lambda i: (i, 0)),
                  pl.BlockSpec((SUBLANES, pw), lambda i: (jnp.maximum(i * rb - 1, 0), 0)),
                  pl.BlockSpec((SUBLANES, pw), lambda i: (jnp.minimum((i + 1) * rb, last), 0)),
                  pl.BlockSpec(pool_w.shape, lambda i: (0, 0, 0)),
                  pl.BlockSpec((1, pw), lambda i: (0, 0))],
        out_specs=pl.BlockSpec((bm, pw), lambda i: (i, 0)),
        out_shape=jax.ShapeDtypeStruct((m, pw), BF16),
        compiler_params=_params("arbitrary"),
        name="multiscale_pool",
    )(u, u, u, pool_w, pool_scale.reshape(1, pw))


def _flash_tiles(q_tiles, k_rows, vt_ref, finish, *, chunks):
    dv = vt_ref.shape[0]

    nk = len(chunks)

    def update(t, s, state):
        m_run, l_run, acc = state
        m_new = jnp.maximum(m_run, jnp.max(s, axis=0, keepdims=True))
        alpha = jnp.exp2(m_run - m_new)
        p = jnp.exp2(s - m_new)
        l_new = alpha * l_run + jnp.sum(p, axis=0, keepdims=True)
        acc = alpha * acc + jnp.dot(vt_ref[:, chunks[t]], p.astype(BF16), preferred_element_type=F32)
        return m_new, l_new, acc

    pending = None
    for n, qs in enumerate(q_tiles):
        r = qs.shape[0]
        scores_of = lambda t, qs=qs: lax.dot_general(k_rows(chunks[t]), qs, (((1,), (1,)), ((), ())),
                                                     preferred_element_type=F32)
        state = (jnp.full((1, r), -jnp.inf, F32), jnp.zeros((1, r), F32), jnp.zeros((dv, r), F32))
        scores = scores_of(0)
        if pending is not None:
            pending()
        for t in range(nk - 1):
            nxt = scores_of(t + 1)
            state = update(t, scores, state)
            scores = nxt

        def pending(n=n, scores=scores, state=state):
            _, l_run, acc = update(nk - 1, scores, state)
            finish(n, l_run, acc)
    pending()


def _diff_attn_kernel(q_ref, k_ref, vt_ref, lq1_ref, lk1_ref, lq2_ref, lk2_ref, g_ref, *rest, bq, tiles, chunks,
                      lam_init):
    o_ref = rest[-1]
    bt = bq // tiles
    lane = lax.broadcasted_iota(jnp.int32, (bt, HEAD_DIM), 1)
    zero = jnp.zeros((bt, HEAD_DIM), q_ref.dtype)
    q_tiles = []
    for n in range(tiles):
        q = q_ref[n * bt:(n + 1) * bt, :]
        q_tiles.append(jnp.concatenate([jnp.where(lane < HEAD_DIM // 2, q, zero),
                                        jnp.where(lane >= HEAD_DIM // 2, q, zero)], axis=0))
    lam = (jnp.exp(jnp.sum(lq1_ref[...] * lk1_ref[...], axis=1, keepdims=True))
           - jnp.exp(jnp.sum(lq2_ref[...] * lk2_ref[...], axis=1, keepdims=True)) + lam_init)

    def finish(n, l_run, acc):
        o_t = acc / l_run
        o = (o_t[:, :bt] - lam * o_t[:, bt:]).T
        y = o * lax.rsqrt(jnp.mean(o * o, axis=-1, keepdims=True) + EPS)
        o_ref[n * bt:(n + 1) * bt, :] = (y * g_ref[...] * (1 - lam_init)).astype(o_ref.dtype)

    _flash_tiles(q_tiles, lambda rows: k_ref[rows, :], vt_ref, finish, chunks=chunks)


DIFF_CHUNK_SHARES = (2, 5, 6, 6, 6, 5, 3)
MLA_CHUNK_SHARES = (5, 10, 10, 8)


def _kv_chunks(kv_rows, shares):
    unit, total = MXU_DIM, sum(shares)
    if kv_rows % (unit * total):
        return (slice(0, kv_rows),)
    sizes = [kv_rows // total * share for share in shares]
    starts = [sum(sizes[:t]) for t in range(len(sizes))]
    return tuple(slice(a, a + n) for a, n in zip(starts, sizes))


def _out_alias(out, n_in):
    return jax.ShapeDtypeStruct(out.shape, out.dtype), [pl.BlockSpec(memory_space=pl.ANY)], [out], {n_in: 0}


def _diff_attn_call(qk, vt, lam_vecs, subln, out, *, heads, q_rows, q_row0, kv_rows, kv_row0, lam_init, name):
    bq = _pick(q_rows, (512, 256, 128))
    tiles = max(1, bq // 256)
    qb0, kb0 = q_row0 // bq, kv_row0 // kv_rows
    vec = pl.BlockSpec((1, HEAD_DIM // 2), lambda h, i: (0, 0))
    out_shape, alias_specs, alias_args, aliases = _out_alias(out, 8)
    return pl.pallas_call(
        functools.partial(_diff_attn_kernel, bq=bq, tiles=tiles, chunks=_kv_chunks(kv_rows, DIFF_CHUNK_SHARES), lam_init=lam_init),
        grid=(heads, q_rows // bq),
        in_specs=[pl.BlockSpec((bq, HEAD_DIM), lambda h, i: (qb0 + i, h)),
                  pl.BlockSpec((kv_rows, HEAD_DIM), lambda h, i: (kb0, heads + h)),
                  pl.BlockSpec((HEAD_DIM, kv_rows), lambda h, i: (h, kb0)),
                  vec, vec, vec, vec,
                  pl.BlockSpec((1, HEAD_DIM), lambda h, i: (0, 0))] + alias_specs,
        out_specs=pl.BlockSpec((bq, HEAD_DIM), lambda h, i: (qb0 + i, h)),
        out_shape=out_shape,
        input_output_aliases=aliases,
        compiler_params=_params("arbitrary", "arbitrary"),
        name=name,
    )(qk, qk, vt, *lam_vecs, subln, *alias_args)


def _mla_attn_kernel(qn_ref, qp_ref, kn_ref, kp_ref, vt_ref, *rest, bq, tiles, chunks):
    o_ref = rest[-1]
    bt = bq // tiles
    q_tiles = [jnp.concatenate([qn_ref[n * bt:(n + 1) * bt, :], qp_ref[n * bt:(n + 1) * bt, :]], axis=1)
               for n in range(tiles)]

    def k_rows(rows):
        return jnp.concatenate([kn_ref[rows, :], kp_ref[rows, :]], axis=1)

    def finish(n, l_run, acc):
        o_ref[n * bt:(n + 1) * bt, :] = (acc / l_run).T.astype(o_ref.dtype)

    _flash_tiles(q_tiles, k_rows, vt_ref, finish, chunks=chunks)


def _mla_attn_call(qn, qp, kn, kpe, vt, out, *, heads, q_rows, q_row0, kv_rows, kv_row0, name):
    bq = _pick(q_rows, (1024, 512, 256, 128))
    tiles = max(1, bq // 512)
    qb0, kb0 = q_row0 // bq, kv_row0 // kv_rows
    out_shape, alias_specs, alias_args, aliases = _out_alias(out, 5)
    return pl.pallas_call(
        functools.partial(_mla_attn_kernel, bq=bq, tiles=tiles, chunks=_kv_chunks(kv_rows, MLA_CHUNK_SHARES)),
        grid=(heads, q_rows // bq),
        in_specs=[pl.BlockSpec((bq, HEAD_DIM), lambda h, i: (qb0 + i, h)),
                  pl.BlockSpec((bq, HEAD_DIM), lambda h, i: (qb0 + i, h)),
                  pl.BlockSpec((kv_rows, HEAD_DIM), lambda h, i: (kb0, h)),
                  pl.BlockSpec((kv_rows, LANES), lambda h, i: (kb0, 0)),
                  pl.BlockSpec((HEAD_DIM, kv_rows), lambda h, i: (h, kb0))] + alias_specs,
        out_specs=pl.BlockSpec((bq, HEAD_DIM), lambda h, i: (qb0 + i, h)),
        out_shape=out_shape,
        input_output_aliases=aliases,
        compiler_params=_params("arbitrary", "arbitrary"),
        name=name,
    )(qn, qp, kn, kpe, vt, *alias_args)


def _rope_tables(seq, ctx_len, q_scale):
    rows = seq // GRID_W
    row = jnp.broadcast_to(jnp.arange(rows, dtype=F32)[:, None], (rows, GRID_W)).reshape(seq)
    col = jnp.broadcast_to(jnp.arange(GRID_W, dtype=F32)[None, :], (rows, GRID_W)).reshape(seq)
    n_freq = 16
    inv_freq = ROPE_BASE ** (-jnp.arange(n_freq, dtype=F32) / n_freq)
    ang = jnp.concatenate([row[:, None] * inv_freq, col[:, None] * inv_freq], axis=-1)
    cos, sin = jnp.cos(ang), jnp.sin(ang)
    zero = jnp.zeros_like(sin)
    c = jnp.concatenate([cos, cos, cos, cos], axis=-1)
    s1 = jnp.concatenate([-sin, zero, -sin, zero], axis=-1)
    s2 = jnp.concatenate([zero, sin, zero, sin], axis=-1)
    pad = lambda t, v: jnp.concatenate([t, jnp.full((ctx_len, LANES), v, F32)], axis=0)
    return tuple(jnp.concatenate([t, t * q_scale], axis=1) for t in (pad(c, 1.0), pad(s1, 0.0), pad(s2, 0.0)))


def kernel(x, c, ctx, c_ctx, ada_down, ada_up, ada_bias, norm_mix, norm_ffn, ffn_gate, ffn_up, ffn_down, ev_w_in, ev_conv, ev_lam_q1, ev_lam_k1, ev_lam_q2, ev_lam_k2, ev_subln, ev_w_out, od_w_in, od_q_norm, od_w_uq, od_kv_norm, od_w_ukv, od_pool_w, od_pool_scale, od_w_out, final_norm):
    _, seq, d = x.shape
    ctx_len = ctx.shape[1]
    m = seq + ctx_len
    depth = ada_down.shape[0]
    half = d // 2
    heads = half // HEAD_DIM
    q_rank = od_q_norm.shape[-1]
    kv_rank = od_kv_norm.shape[-1]
    rope_dim = HEAD_DIM // 2
    bf = lambda t: t.astype(BF16)

    diff_q_scale = (HEAD_DIM // 2) ** -0.5 * LOG2E
    mla_q_scale = (HEAD_DIM + rope_dim) ** -0.5 * LOG2E
    tables_diff = _rope_tables(seq, ctx_len, diff_q_scale)
    tables_mla = _rope_tables(seq, ctx_len, mla_q_scale)

    cc = jnp.zeros((SUBLANES, d), F32).at[0].set(c[0]).at[1].set(c_ctx)
    mods = _ada_call(cc, ada_down, ada_up, ada_bias.reshape(depth, 1, -1))

    for l in range(depth):
        mod = mods[l]
        i = l // 2
        if l == 0:
            h, r = _norm_join_call(x[0], ctx[0], norm_mix[l], mod, shift_blk=0, scale_blk=1)
        else:
            h = _norm_call(r, norm_mix[l], rows=m, out_dtype=BF16, mod=mod, shift_blk=0, scale_blk=1, n_lat=seq,
                           name="norm_mod_mix")
        if l % 2 == 0:
            lam_init = 0.8 - 0.6 * math.exp(-0.3 * l)
            big_rows = (1408, 1280, 768, 640, 512, 256, 128)
            pa = _wmm_call([h], [(ev_w_in, i, 0)], n=3 * half, out_dtype=F32, bm_cands=big_rows, name="ev_in_conv")
            qk = _wmm_call([h], [(ev_w_in, i, 0)], n=2 * half, col0=3 * half, out_dtype=BF16, epi="rope",
                           rope=tables_diff, rope_cols=(0, 2 * half), rope_q_cols=(0, half), bm_cands=big_rows,
                           name="ev_in_qk")
            vt = _wmm_call([h], [(ev_w_in, i, 0)], n=half, col0=5 * half, out_dtype=BF16, epi="transpose",
                           bm_cands=big_rows, name="ev_in_vt")
            ya = _conv_call(pa, ev_conv[i], n_lat=seq)
            lam_vecs = [v[i].reshape(1, -1) for v in (ev_lam_q1, ev_lam_k1, ev_lam_q2, ev_lam_k2)]
            subln = ev_subln[i].reshape(1, -1)
            yb = jnp.zeros((m, half), BF16)
            yb = _diff_attn_call(qk, vt, lam_vecs, subln, yb, heads=heads, q_rows=seq, q_row0=0, kv_rows=m,
                                 kv_row0=0, lam_init=lam_init, name="diff_attn")
            yb = _diff_attn_call(qk, vt, lam_vecs, subln, yb, heads=heads, q_rows=ctx_len, q_row0=seq,
                                 kv_rows=ctx_len, kv_row0=seq, lam_init=lam_init, name="diff_attn_ctx")
            y1, y2 = ya, yb
            w_out = ev_w_out
        else:
            w_in = od_w_in[i]
            w_kpe = jnp.pad(w_in[:, q_rank + kv_rank:q_rank + kv_rank + rope_dim], ((0, 0), (0, LANES - rope_dim)))
            cq = _mm_call(h, bf(w_in[:, :q_rank]), out_dtype=F32, name="od_in_cq")
            ckv = _mm_call(h, bf(w_in[:, q_rank:q_rank + kv_rank]), out_dtype=F32, name="od_in_ckv")
            kpe = _wmm_call([h], [(w_kpe, None, 0)], n=LANES, out_dtype=BF16, epi="rope", rope=tables_mla,
                            rope_cols=(0, LANES), name="od_in_kpe")
            u = _mm_call(h, bf(w_in[:, q_rank + kv_rank + rope_dim:]), out_dtype=F32, name="od_in_pool")
            w_uq = od_w_uq[i].reshape(q_rank, heads, HEAD_DIM + rope_dim)
            w_qn = w_uq[:, :, :HEAD_DIM].reshape(q_rank, half)
            w_qp = jnp.pad(w_uq[:, :, HEAD_DIM:], ((0, 0), (0, 0), (0, LANES - rope_dim))).reshape(q_rank, heads * LANES)
            wide = (2048, 1024, 512, 256, 128)
            qn = _wmm_call([cq], [(w_qn, None, 0)], n=half, out_dtype=BF16, out_scale=mla_q_scale, bn_cands=wide,
                           a_norm=od_q_norm[i], name="od_uq_nope")
            qp = _wmm_call([cq], [(w_qp, None, 0)], n=heads * LANES, out_dtype=BF16, epi="rope", rope=tables_mla,
                           rope_cols=(0, heads * LANES), rope_q_cols=(0, heads * LANES), bn_cands=wide,
                           a_norm=od_q_norm[i], name="od_uq_rope")
            w_ukv = od_w_ukv[i].reshape(kv_rank, heads, 2 * HEAD_DIM)
            kn = _wmm_call([ckv], [(w_ukv[:, :, :HEAD_DIM].reshape(kv_rank, half), None, 0)], n=half, out_dtype=BF16,
                           bn_cands=wide, a_norm=od_kv_norm[i], name="od_uk")
            vt = _wmm_call([ckv], [(w_ukv[:, :, HEAD_DIM:].reshape(kv_rank, half), None, 0)], n=half, out_dtype=BF16,
                           epi="transpose", bn_cands=wide, a_norm=od_kv_norm[i], name="od_uvt")
            ym = jnp.zeros((m, half), BF16)
            ym = _mla_attn_call(qn, qp, kn, kpe, vt, ym, heads=heads, q_rows=seq, q_row0=0, kv_rows=m, kv_row0=0,
                                name="mla_attn")
            ym = _mla_attn_call(qn, qp, kn, kpe, vt, ym, heads=heads, q_rows=ctx_len, q_row0=seq, kv_rows=ctx_len,
                                kv_row0=seq, name="mla_attn_ctx")
            y1 = _pool_call(u, bf(od_pool_w[i]), od_pool_scale[i], n_lat=seq)
            y2 = ym
            w_out = od_w_out
        r = _wmm_call([y1, y2], [(w_out, i, 0), (w_out, i, 1)], n=d, out_dtype=F32, epi="res", res=r, mod=mod,
                      gate_blk=2, n_lat=seq, name="mix_out")
        h2 = _norm_call(r, norm_ffn[l], rows=m, out_dtype=BF16, mod=mod, shift_blk=3, scale_blk=4, n_lat=seq,
                        name="norm_mod_ffn")
        a, w_down = _wmm_call([h2], [(ffn_gate, l, 0), (ffn_up, l, 0)], n=ffn_gate.shape[-1], out_dtype=BF16,
                              epi="swiglu", bm_cands=(1408, 1280, 768, 640, 512, 256, 128), bn_cands=(256, 128),
                              side_cast=(ffn_down, l), name="ffn_gate_up")
        r = _mm_res_call(a, w_down, r, mod, 5, n_lat=seq, name="ffn_down")
    out = _norm_call(r, final_norm, rows=seq, out_dtype=F32, name="final_norm")
    return out[None]
```
